```python
import math
import jax, jax.numpy as jnp
from jax import lax
import numpy as np

D_MODEL = 1024
BATCH = 4
SEQ = 8192
DEPTH = 4

D_MIX = 1024
A_HEADS = 8
A_HEAD_DIM = 64
A_WIDTH = A_HEADS * A_HEAD_DIM
IDX_HEADS = 8
IDX_DIM = 64
TOPK_MAX = 256
Q_BLOCK = 128
B_WIDTH = 256
CONV_WIDTH = 31
C_HEADS = 4
C_KEY_DIM = 32
C_VAL_DIM = 64
C_WIDTH = C_HEADS * C_VAL_DIM
GATE_RANK = 16
GATE_TAU = 16.0
GLA_CHUNK = 64
ROPE_THETA = 10000.0
ROPE_DIM = 64
DEEPNORM_ALPHA = (2 * DEPTH) ** 0.25
DEEPNORM_BETA = (8 * DEPTH) ** -0.25
EPS = 1e-5

IN_SIZES = (A_WIDTH, A_WIDTH, A_WIDTH, A_WIDTH,
            IDX_HEADS * IDX_DIM, IDX_DIM, IDX_HEADS,
            2 * B_WIDTH, B_WIDTH,
            C_HEADS * C_KEY_DIM, C_HEADS * C_KEY_DIM,
            C_WIDTH, C_WIDTH, GATE_RANK)
D_IN = sum(IN_SIZES)

kernel_name = "hybrid_dsa_conformer_gla_deepnorm"


def _split_in(h):
    points, acc = [], 0
    for s in IN_SIZES[:-1]:
        acc += s
        points.append(acc)
    return jnp.split(h, points, axis=-1)


def _layernorm(x, g, b):
    xf = x.astype(jnp.float32)
    mu = jnp.mean(xf, axis=-1, keepdims=True)
    var = jnp.mean(jnp.square(xf - mu), axis=-1, keepdims=True)
    y = (xf - mu) * lax.rsqrt(var + EPS) * g.astype(jnp.float32) + b.astype(jnp.float32)
    return y.astype(x.dtype)


def _rope_tables(positions):
    inv = ROPE_THETA ** (-jnp.arange(0, ROPE_DIM, 2, dtype=jnp.float32) / ROPE_DIM)
    ang = positions.astype(jnp.float32)[..., None] * inv
    return jnp.cos(ang), jnp.sin(ang)


def _rope(x, cos, sin):
    x1, x2 = jnp.split(x, 2, axis=-1)
    c = cos[:, :, None, :].astype(x.dtype)
    s = sin[:, :, None, :].astype(x.dtype)
    return jnp.concatenate([x1 * c - x2 * s, x2 * c + x1 * s], axis=-1)


def _dsa_attention(q, k, v, qi, ki, wi):
    B, T, H, dh = q.shape
    topk = min(TOPK_MAX, T // 4)
    nb = T // Q_BLOCK
    scale = dh ** -0.5
    s_pos = jnp.arange(T)

    def to_blocks(a):
        return jnp.moveaxis(a.reshape((B, nb, Q_BLOCK) + a.shape[2:]), 1, 0)

    def block(args):
        qb, qib, wib, start = args
        t_pos = start + jnp.arange(Q_BLOCK)
        causal = s_pos[None, :] <= t_pos[:, None]
        idx_logits = jnp.einsum('bqhd,bsd->bqhs', qib, ki)
        score = jnp.einsum('bqh,bqhs->bqs', wib, jax.nn.relu(idx_logits)).astype(jnp.float32)
        score = jnp.where(causal[None], score, -jnp.inf)
        _, sel = lax.top_k(score, topk)
        k_sel = jax.vmap(lambda kb, ib: kb[ib])(k, sel)
        v_sel = jax.vmap(lambda vb, ib: vb[ib])(v, sel)
        logits = jnp.einsum('bqhd,bqkhd->bhqk', qb, k_sel).astype(jnp.float32) * scale
        valid = sel <= t_pos[None, :, None]
        logits = jnp.where(valid[:, None], logits, -jnp.inf)
        p = jax.nn.softmax(logits, axis=-1).astype(v.dtype)
        return jnp.einsum('bhqk,bqkhd->bqhd', p, v_sel)

    starts = jnp.arange(nb) * Q_BLOCK
    out = lax.map(block, (to_blocks(q), to_blocks(qi), to_blocks(wi), starts))
    return jnp.moveaxis(out, 0, 1).reshape(B, T, H, dh)


def _conformer_conv(u, conv_w, conv_b, ln_g, ln_b, pw_w, pw_b):
    val, gate = jnp.split(u, 2, axis=-1)
    h = val * jax.nn.sigmoid(gate)
    h = lax.conv_general_dilated(h, conv_w[:, None, :].astype(h.dtype), window_strides=(1,),
                                 padding=((CONV_WIDTH - 1, 0),),
                                 dimension_numbers=('NWC', 'WIO', 'NWC'),
                                 feature_group_count=B_WIDTH) + conv_b
    h = jax.nn.silu(_layernorm(h, ln_g, ln_b))
    return h @ pw_w + pw_b


def _gla(q, k, v, log_a):
    B, T, H, dk = q.shape
    dv = v.shape[-1]
    C = GLA_CHUNK
    n = T // C

    def to_chunks(a):
        return a.reshape(B, n, C, H, a.shape[-1]).transpose(1, 0, 3, 2, 4).astype(jnp.float32)

    qc, kc, vc, gc = to_chunks(q * (dk ** -0.5)), to_chunks(k), to_chunks(v), to_chunks(log_a)
    causal = jnp.tril(jnp.ones((C, C), dtype=bool))[None, None, :, :, None]

    def step(S, inp):
        qb, kb, vb, gb = inp
        b = jnp.cumsum(gb, axis=2)
        o_inter = jnp.einsum('bhcd,bhde->bhce', qb * jnp.exp(b), S)
        diff = b[:, :, :, None, :] - b[:, :, None, :, :]
        decay = jnp.exp(jnp.where(causal, diff, -jnp.inf))
        A = jnp.einsum('bhid,bhjd,bhijd->bhij', qb, kb, decay)
        o_intra = jnp.einsum('bhij,bhje->bhie', A, vb)
        b_last = b[:, :, -1:, :]
        S_new = jnp.exp(b_last[:, :, 0, :])[..., None] * S + \
            jnp.einsum('bhcd,bhce->bhde', kb * jnp.exp(b_last - b), vb)
        return S_new, o_inter + o_intra

    S0 = jnp.zeros((B, H, dk, dv), jnp.float32)
    _, o = lax.scan(step, S0, (qc, kc, vc, gc))
    return o.transpose(1, 0, 3, 2, 4).reshape(B, T, H, dv)


def _layer(x, cos, sin, w_in, conv_w, conv_b, cln_g, cln_b, pw_w, pw_b,
           gate_w2, gate_b, gnorm_g, w_out, ln_g, ln_b):
    B, T, _ = x.shape
    h = x @ w_in
    (a_q, a_k, a_v, a_g, i_q, i_k, i_w, b_glu, b_g,
     c_q, c_k, c_v, c_g, c_lr) = _split_in(h)

    q = _rope(a_q.reshape(B, T, A_HEADS, A_HEAD_DIM), cos, sin)
    k = _rope(a_k.reshape(B, T, A_HEADS, A_HEAD_DIM), cos, sin)
    v = a_v.reshape(B, T, A_HEADS, A_HEAD_DIM)
    qi = _rope(i_q.reshape(B, T, IDX_HEADS, IDX_DIM), cos, sin)
    ki = _rope(i_k[:, :, None, :], cos, sin)[:, :, 0, :]
    wi = i_w * (IDX_HEADS ** -0.5 * IDX_DIM ** -0.5)
    y_a = _dsa_attention(q, k, v, qi, ki, wi).reshape(B, T, A_WIDTH) * jax.nn.silu(a_g)

    y_b = _conformer_conv(b_glu, conv_w, conv_b, cln_g, cln_b, pw_w, pw_b) * jax.nn.silu(b_g)

    log_a = jax.nn.log_sigmoid((c_lr @ gate_w2 + gate_b).astype(jnp.float32)) / GATE_TAU
    o = _gla(c_q.reshape(B, T, C_HEADS, C_KEY_DIM), c_k.reshape(B, T, C_HEADS, C_KEY_DIM),
             c_v.reshape(B, T, C_HEADS, C_VAL_DIM), log_a.reshape(B, T, C_HEADS, C_KEY_DIM))
    o = o * lax.rsqrt(jnp.mean(jnp.square(o), axis=-1, keepdims=True) + EPS) * \
        gnorm_g.astype(jnp.float32).reshape(C_HEADS, C_VAL_DIM)
    y_c = o.reshape(B, T, C_WIDTH).astype(x.dtype) * jax.nn.silu(c_g)

    y = jnp.concatenate([y_a.astype(x.dtype), y_b.astype(x.dtype), y_c], axis=-1) @ w_out
    return _layernorm(DEEPNORM_ALPHA * x + y, ln_g, ln_b)


def setup_inputs(seed: int = 0) -> dict:
    key = jax.random.key(seed)
    ks = jax.random.split(key, 16)

    def nrm(k, shape, scale):
        return jax.random.normal(k, shape, jnp.float32) * scale

    x = nrm(ks[0], (BATCH, SEQ, D_MODEL), 1.0)
    positions = jnp.broadcast_to(jnp.arange(SEQ, dtype=jnp.int32), (BATCH, SEQ))
    w_in = nrm(ks[1], (DEPTH, D_MODEL, D_IN), D_MODEL ** -0.5)
    conv_w = nrm(ks[2], (DEPTH, CONV_WIDTH, B_WIDTH), CONV_WIDTH ** -0.5)
    conv_b = nrm(ks[3], (DEPTH, B_WIDTH), 0.02)
    cln_g = 1.0 + nrm(ks[4], (DEPTH, B_WIDTH), 0.05)
    cln_b = nrm(ks[5], (DEPTH, B_WIDTH), 0.02)
    pw_w = nrm(ks[6], (DEPTH, B_WIDTH, B_WIDTH), B_WIDTH ** -0.5)
    pw_b = nrm(ks[7], (DEPTH, B_WIDTH), 0.02)
    gate_w2 = nrm(ks[8], (DEPTH, GATE_RANK, C_HEADS * C_KEY_DIM), GATE_RANK ** -0.5)
    gate_b = nrm(ks[9], (DEPTH, C_HEADS * C_KEY_DIM), 0.1)
    gnorm_g = 1.0 + nrm(ks[10], (DEPTH, C_WIDTH), 0.05)
    w_out = nrm(ks[11], (DEPTH, D_MIX, D_MODEL), DEEPNORM_BETA * D_MIX ** -0.5)
    ln_g = 1.0 + nrm(ks[12], (DEPTH, D_MODEL), 0.05)
    ln_b = nrm(ks[13], (DEPTH, D_MODEL), 0.02)
    return {"x": x, "positions": positions, "w_in": w_in, "conv_w": conv_w, "conv_b": conv_b,
            "cln_g": cln_g, "cln_b": cln_b, "pw_w": pw_w, "pw_b": pw_b,
            "gate_w2": gate_w2, "gate_b": gate_b, "gnorm_g": gnorm_g,
            "w_out": w_out, "ln_g": ln_g, "ln_b": ln_b}


def reference(x, positions, w_in, conv_w, conv_b, cln_g, cln_b, pw_w, pw_b,
              gate_w2, gate_b, gnorm_g, w_out, ln_g, ln_b):
    cos, sin = _rope_tables(positions)
    for i in range(DEPTH):
        x = _layer(x, cos, sin, w_in[i], conv_w[i], conv_b[i], cln_g[i], cln_b[i],
                   pw_w[i], pw_b[i], gate_w2[i], gate_b[i], gnorm_g[i],
                   w_out[i], ln_g[i], ln_b[i])
    return x
```

```python
import functools

import numpy as np
import jax
import jax.numpy as jnp
from jax import lax
from jax.experimental import pallas as pl
from jax.experimental.pallas import tpu as pltpu

F32 = jnp.float32
BF16 = jnp.bfloat16
I32 = jnp.int32

A_HEADS = 8
A_HEAD_DIM = 64
A_WIDTH = A_HEADS * A_HEAD_DIM
IDX_HEADS = 8
IDX_DIM = 64
TOPK_MAX = 256
B_WIDTH = 256
CONV_WIDTH = 31
C_HEADS = 4
C_KEY_DIM = 32
C_VAL_DIM = 64
C_KEYS = C_HEADS * C_KEY_DIM
C_WIDTH = C_HEADS * C_VAL_DIM
GATE_RANK = 16
GATE_TAU = 16.0
GLA_CHUNK = 64
ROPE_THETA = 10000.0
ROPE_HALF = 32
EPS = 1e-5

LANES = 128
VMEM_LIMIT = 56 * 1024 * 1024
INT_MIN = int(np.iinfo(np.int32).min)
NEG_BIG = -1e30

R_K, R_KI, R_AG, R_GV, R_GG, R_BG, R_CQ, R_CK, R_CV, R_CG, R_LR, R_END = (
    0, 512, 640, 1152, 1408, 1664, 1920, 2048, 2176, 2432, 2688, 2816)
T_Q, T_QI, T_V, T_WI, T_END = 0, 512, 1024, 1536, 1552


def _dot(a, b):
    return jnp.dot(a, b, preferred_element_type=F32)


def _dot_nt(a, b):
    return lax.dot_general(a, b, (((1,), (1,)), ((), ())), preferred_element_type=F32)


def _sigmoid(x):
    return 1.0 / (1.0 + jnp.exp(-x))


def _silu(x):
    return x * _sigmoid(x)


def _split3(x):
    a = x.astype(BF16)
    r = x - a.astype(F32)
    b = r.astype(BF16)
    c = (r - b.astype(F32)).astype(BF16)
    return a, b, c


def _rope_kernel(post_ref, posc_ref, invc_ref, invr_ref, sgn_ref,
                 cost_ref, sint_ref, cosr_ref, sinr_ref):
    ang_t = invc_ref[...] * post_ref[...]
    cost_ref[...] = jnp.cos(ang_t)
    sint_ref[...] = jnp.sin(ang_t)
    ang_r = posc_ref[...] * invr_ref[...]
    cosr_ref[...] = jnp.cos(ang_r)
    sinr_ref[...] = jnp.sin(ang_r) * sgn_ref[...]


def _rope_tables(positions):
    m = positions.size
    tm = 1024
    pos = positions.reshape(-1).astype(F32)
    inv = ROPE_THETA ** (-jnp.arange(0, 2 * ROPE_HALF, 2, dtype=F32) / (2 * ROPE_HALF))
    lane = np.arange(LANES)
    sgn = jnp.asarray(np.where(lane % 64 < ROPE_HALF, -1.0, 1.0), F32).reshape(1, LANES)
    inv_r = jnp.tile(inv, LANES // ROPE_HALF).reshape(1, LANES)
    return pl.pallas_call(
        _rope_kernel,
        grid=(m // tm,),
        in_specs=[pl.BlockSpec((1, tm), lambda i: (0, i)),
                  pl.BlockSpec((tm, 1), lambda i: (i, 0)),
                  pl.BlockSpec((ROPE_HALF, 1), lambda i: (0, 0)),
                  pl.BlockSpec((1, LANES), lambda i: (0, 0)),
                  pl.BlockSpec((1, LANES), lambda i: (0, 0))],
        out_specs=[pl.BlockSpec((ROPE_HALF, tm), lambda i: (0, i)),
                   pl.BlockSpec((ROPE_HALF, tm), lambda i: (0, i)),
                   pl.BlockSpec((tm, LANES), lambda i: (i, 0)),
                   pl.BlockSpec((tm, LANES), lambda i: (i, 0))],
        out_shape=[jax.ShapeDtypeStruct((ROPE_HALF, m), F32),
                   jax.ShapeDtypeStruct((ROPE_HALF, m), F32),
                   jax.ShapeDtypeStruct((m, LANES), F32),
                   jax.ShapeDtypeStruct((m, LANES), F32)],
        name="rope_tables",
    )(pos.reshape(1, m), pos.reshape(m, 1), inv.reshape(ROPE_HALF, 1), inv_r, sgn)


def _inproj_kernel(x_ref, wt_ref, wr_ref, w2h_ref, w2l_ref, gb_ref,
                   cost_ref, sint_ref, cosr_ref, sinr_ref,
                   qt_ref, qit_ref, vt_ref, wit_ref, k_ref, ki_ref, sag_ref,
                   glu_ref, sbg_ref, cq_ref, ck_ref, cv_ref, scg_ref, la_ref):
    xb = x_ref[...].astype(BF16)
    ct = cost_ref[...]
    st = sint_ref[...]

    ht = _dot_nt(wt_ref[...], xb)
    for h in range(A_HEADS):
        for base, out, scale in ((T_Q, qt_ref, A_HEAD_DIM ** -0.5), (T_QI, qit_ref, 1.0)):
            r0 = base + h * 64
            x1 = ht[r0:r0 + 32]
            x2 = ht[r0 + 32:r0 + 64]
            o0 = h * 64
            out[o0:o0 + 32, :] = ((x1 * ct - x2 * st) * scale).astype(BF16)
            out[o0 + 32:o0 + 64, :] = ((x2 * ct + x1 * st) * scale).astype(BF16)
    vt_ref[...] = ht[T_V:T_V + A_WIDTH].astype(BF16)
    wit_ref[...] = ht[T_WI:T_WI + IDX_HEADS] * (IDX_HEADS ** -0.5 * IDX_DIM ** -0.5)

    cr = cosr_ref[...]
    sr = sinr_ref[...]
    lane = lax.broadcasted_iota(I32, cr.shape, 1)
    first_half = (lane % 64) < ROPE_HALF

    def rope_rows(v):
        fwd = pltpu.roll(v, LANES - ROPE_HALF, 1)
        bwd = pltpu.roll(v, ROPE_HALF, 1)
        return v * cr + jnp.where(first_half, fwd, bwd) * sr

    for g in range(A_WIDTH // LANES):
        c0 = R_K + g * LANES
        hk = _dot(xb, wr_ref[:, c0:c0 + LANES])
        k_ref[:, g * LANES:(g + 1) * LANES] = rope_rows(hk).astype(BF16)
    hki = _dot(xb, wr_ref[:, R_KI:R_KI + LANES])
    ki_ref[...] = rope_rows(hki)[:, :IDX_DIM].astype(BF16)

    sag_ref[...] = _silu(_dot(xb, wr_ref[:, R_AG:R_AG + A_WIDTH])).astype(BF16)
    gv = _dot(xb, wr_ref[:, R_GV:R_GV + B_WIDTH])
    gg = _dot(xb, wr_ref[:, R_GG:R_GG + B_WIDTH])
    glu_ref[...] = gv * _sigmoid(gg)
    sbg_ref[...] = _silu(_dot(xb, wr_ref[:, R_BG:R_BG + B_WIDTH])).astype(BF16)
    cq_ref[...] = _dot(xb, wr_ref[:, R_CQ:R_CQ + C_KEYS]) * (C_KEY_DIM ** -0.5)
    ck_ref[...] = _dot(xb, wr_ref[:, R_CK:R_CK + C_KEYS])
    cv_ref[...] = _dot(xb, wr_ref[:, R_CV:R_CV + C_WIDTH])
    scg_ref[...] = _silu(_dot(xb, wr_ref[:, R_CG:R_CG + C_WIDTH])).astype(BF16)

    lr = _dot(xb, wr_ref[:, R_LR:R_LR + LANES])
    lr_h = lr.astype(BF16)
    lr_l = (lr - lr_h.astype(F32)).astype(BF16)
    z = (_dot(lr_h, w2h_ref[...]) + _dot(lr_l, w2h_ref[...]) + _dot(lr_h, w2l_ref[...])
         + gb_ref[...])
    log_sig = jnp.minimum(z, 0.0) - jnp.log(1.0 + jnp.exp(-jnp.abs(z)))
    la_ref[...] = log_sig * (1.0 / GATE_TAU)


def _inproj(x2, wt, wr, w2h, w2l, gb, tables):
    m, d = x2.shape
    tm = 512
    cost, sint, cosr, sinr = tables
    row = lambda w: pl.BlockSpec((tm, w), lambda i: (i, 0))
    col = lambda r: pl.BlockSpec((r, tm), lambda i: (0, i))
    full = lambda a: pl.BlockSpec(a.shape, lambda i: (0, 0))
    outs = [(col(A_WIDTH), (A_WIDTH, m), BF16),
            (col(A_WIDTH), (A_WIDTH, m), BF16),
            (col(A_WIDTH), (A_WIDTH, m), BF16),
            (col(IDX_HEADS), (IDX_HEADS, m), F32),
            (row(A_WIDTH), (m, A_WIDTH), BF16),
            (row(IDX_DIM), (m, IDX_DIM), BF16),
            (row(A_WIDTH), (m, A_WIDTH), BF16),
            (row(B_WIDTH), (m, B_WIDTH), F32),
            (row(B_WIDTH), (m, B_WIDTH), BF16),
            (row(C_KEYS), (m, C_KEYS), F32),
            (row(C_KEYS), (m, C_KEYS), F32),
            (row(C_WIDTH), (m, C_WIDTH), F32),
            (row(C_WIDTH), (m, C_WIDTH), BF16),
            (row(C_KEYS), (m, C_KEYS), F32)]
    return pl.pallas_call(
        _inproj_kernel,
        grid=(m // tm,),
        in_specs=[row(d), full(wt), full(wr), full(w2h), full(w2l), full(gb),
                  col(ROPE_HALF), col(ROPE_HALF), row(LANES), row(LANES)],
        out_specs=[o[0] for o in outs],
        out_shape=[jax.ShapeDtypeStruct(o[1], o[2]) for o in outs],
        compiler_params=pltpu.CompilerParams(
            dimension_semantics=("arbitrary",), vmem_limit_bytes=VMEM_LIMIT),
        name="in_proj",
    )(x2, wt, wr, w2h, w2l, gb, cost, sint, cosr, sinr)


KEY_LOWEST_FINITE = -2139095040


def _key_to_float(key):
    return pltpu.bitcast(key ^ ((key >> 31) & 0x7FFFFFFF), F32)


def _attn_kernel(qt_ref, qit_ref, wit_ref, sag_ref, k_ref, ki_ref, vt_ref,
                 o_ref, sc_ref, qm_ref, acc_ref, cut_ref, s_ref, p_ref, *, tq, topk, seq):
    i = pl.program_id(1)
    nchunks = i + 1
    tk = tq
    rel = (lax.broadcasted_iota(I32, (tk, tq), 0) - lax.broadcasted_iota(I32, (tk, tq), 1))

    def score_chunk(j, carry):
        ks = pl.multiple_of(j * tk, tk)
        kic = ki_ref[pl.ds(ks, tk), :]
        sc = jnp.zeros((tk, tq), F32)
        for h in range(IDX_HEADS):
            lg = _dot(kic, qit_ref[h * IDX_DIM:(h + 1) * IDX_DIM, :])
            sc = sc + wit_ref[h:h + 1, :] * jnp.maximum(lg, 0.0)
        sc_ref[pl.ds(ks, tk), :] = jnp.where(rel <= (i - j) * tq, sc, -jnp.inf)
        return carry

    lax.fori_loop(0, nchunks, score_chunk, 0)

    def count(hits):
        def body(j, acc):
            ks = pl.multiple_of(j * tk, tk)
            hit = hits(sc_ref[pl.ds(ks, tk), :], ks)
            return acc + hit.reshape(tk // 8, 8, tq).sum(axis=0)
        acc = lax.fori_loop(0, nchunks, body, jnp.zeros((8, tq), I32))
        return acc.sum(axis=0, keepdims=True)

    def count_ge(thr):
        return count(lambda s, ks: jnp.where(s >= thr, 1, 0))

    zero = jnp.zeros((1, tq), I32)
    cnt0 = count_ge(jnp.zeros((1, tq), F32))
    tau = jnp.where(cnt0 >= topk, zero, INT_MIN)
    cnt_tau = jnp.where(cnt0 >= topk, cnt0, zero)

    def bit_step(it, carry):
        tau, cnt_tau = carry
        cand = tau + jnp.left_shift(jnp.int32(1), 30 - it)
        cnt = count_ge(_key_to_float(cand))
        take = cnt >= topk
        return jnp.where(take, cand, tau), jnp.where(take, cnt, cnt_tau)

    tau, cnt_tau = lax.fori_loop(0, 31, bit_step, (tau, cnt_tau))
    thr = _key_to_float(jnp.maximum(tau, KEY_LOWEST_FINITE))

    has_ties = jnp.max(cnt_tau) > topk

    @pl.when(jnp.logical_not(has_ties))
    def _():
        def bias_chunk(j, carry):
            ks = pl.multiple_of(j * tk, tk)
            sc_ref[pl.ds(ks, tk), :] = jnp.where(sc_ref[pl.ds(ks, tk), :] >= thr, 0.0, NEG_BIG)
            return carry

        lax.fori_loop(0, nchunks, bias_chunk, 0)

    @pl.when(has_ties)
    def _():
        need = topk - count(lambda s, ks: jnp.where(s > thr, 1, 0))
        row = lax.broadcasted_iota(I32, (tk, tq), 0)
        nbits = (seq - 1).bit_length()

        def idx_step(it, x):
            cand = x + jnp.left_shift(jnp.int32(1), nbits - 1 - it)
            below = count(lambda s, ks: jnp.where(s == thr, jnp.where(row + ks < cand, 1, 0), 0))
            return jnp.where(below < need, cand, x)

        cut_ref[...] = lax.fori_loop(0, nbits, idx_step, zero)

        def bias_chunk(j, carry):
            ks = pl.multiple_of(j * tk, tk)
            s = sc_ref[pl.ds(ks, tk), :]
            tied = jnp.where(row + ks <= cut_ref[...], 0.0, NEG_BIG)
            sc_ref[pl.ds(ks, tk), :] = jnp.where(s > thr, 0.0, jnp.where(s == thr, tied, NEG_BIG))
            return carry

        lax.fori_loop(0, nchunks, bias_chunk, 0)

    head_row = lax.broadcasted_iota(I32, (LANES, tq), 0) // A_HEAD_DIM
    for h in range(A_HEADS):
        qpair = qt_ref[(h // 2) * LANES:(h // 2 + 1) * LANES, :]
        qm_ref[h] = jnp.where(head_row == (h % 2), qpair, jnp.zeros_like(qpair))
    acc_ref[...] = jnp.zeros(acc_ref.shape, F32)

    def attn_chunk(j, carry):
        ms, ls = carry
        ks = pl.multiple_of(j * tk, tk)
        base = (j % 2) * A_HEADS
        bias = sc_ref[pl.ds(ks, tk), :]
        new_m, new_l, alphas = [], [], []
        for h in range(A_HEADS):
            kc = k_ref[pl.ds(ks, tk), (h // 2) * LANES:(h // 2 + 1) * LANES]
            s = _dot(kc, qm_ref[h]) + bias
            s_ref[base + h] = s
            new_m.append(jnp.maximum(ms[h], s.max(axis=0, keepdims=True)))
        for h in range(A_HEADS):
            p = jnp.exp(s_ref[base + h] - new_m[h])
            p_ref[base + h] = p.astype(BF16)
            alphas.append(jnp.exp(ms[h] - new_m[h]))
            new_l.append(alphas[h] * ls[h] + p.sum(axis=0, keepdims=True))
        for h in range(A_HEADS):
            hs = slice(h * A_HEAD_DIM, (h + 1) * A_HEAD_DIM)
            pv = _dot(vt_ref[hs, pl.ds(ks, tk)], p_ref[base + h])
            acc_ref[hs, :] = alphas[h] * acc_ref[hs, :] + pv
        return tuple(new_m), tuple(new_l)

    init = (tuple(jnp.full((1, tq), NEG_BIG, F32) for _ in range(A_HEADS)),
            tuple(jnp.zeros((1, tq), F32) for _ in range(A_HEADS)))
    _, l_fin = lax.fori_loop(0, nchunks, attn_chunk, init)

    for h in range(A_HEADS):
        hs = slice(h * A_HEAD_DIM, (h + 1) * A_HEAD_DIM)
        acc_ref[hs, :] = acc_ref[hs, :] / l_fin[h]
    o_ref[...] = (acc_ref[...].T * sag_ref[...].astype(F32)).astype(BF16)


def _attention(qt, qit, wit, sag, k, ki, vt, batch, seq):
    tq = 256
    nq = seq // tq
    topk = min(TOPK_MAX, seq // 4)
    once = pl.Buffered(1)
    return pl.pallas_call(
        functools.partial(_attn_kernel, tq=tq, topk=topk, seq=seq),
        grid=(batch, nq),
        in_specs=[pl.BlockSpec((A_WIDTH, tq), lambda b, i: (0, b * nq + i)),
                  pl.BlockSpec((A_WIDTH, tq), lambda b, i: (0, b * nq + i)),
                  pl.BlockSpec((IDX_HEADS, tq), lambda b, i: (0, b * nq + i)),
                  pl.BlockSpec((tq, A_WIDTH), lambda b, i: (b * nq + i, 0)),
                  pl.BlockSpec((seq, A_WIDTH), lambda b, i: (b, 0), pipeline_mode=once),
                  pl.BlockSpec((seq, IDX_DIM), lambda b, i: (b, 0), pipeline_mode=once),
                  pl.BlockSpec((A_WIDTH, seq), lambda b, i: (0, b), pipeline_mode=once)],
        out_specs=pl.BlockSpec((tq, A_WIDTH), lambda b, i: (b * nq + i, 0)),
        out_shape=jax.ShapeDtypeStruct((batch * seq, A_WIDTH), BF16),
        scratch_shapes=[pltpu.VMEM((seq, tq), F32),
                        pltpu.VMEM((A_HEADS, LANES, tq), BF16),
                        pltpu.VMEM((A_WIDTH, tq), F32),
                        pltpu.VMEM((1, tq), I32),
                        pltpu.VMEM((2 * A_HEADS, tq, tq), F32),
                        pltpu.VMEM((2 * A_HEADS, tq, tq), BF16)],
        compiler_params=pltpu.CompilerParams(
            dimension_semantics=("arbitrary", "arbitrary"), vmem_limit_bytes=VMEM_LIMIT),
        name="dsa_attention",
    )(qt, qit, wit, sag, k, ki, vt)


CONV_HALO = 32
CONV_SUB = 64


def _conv_kernel(cur_ref, halo_ref, sbg_ref, cw_ref, cb_ref, lg_ref, lb_ref, pw_ref, pb_ref,
                 o_ref, buf_ref, *, tt):
    i = pl.program_id(1)
    halo = halo_ref[...]
    buf_ref[0:CONV_HALO, :] = jnp.where(i == 0, jnp.zeros_like(halo), halo)
    buf_ref[CONV_HALO:CONV_HALO + tt, :] = cur_ref[...]
    lead = CONV_HALO - (CONV_WIDTH - 1)
    for r in range(tt // CONV_SUB):
        acc = jnp.zeros((CONV_SUB, B_WIDTH), F32)
        for j in range(CONV_WIDTH):
            start = r * CONV_SUB + lead + j
            acc = acc + buf_ref[start:start + CONV_SUB, :] * cw_ref[j:j + 1, :]
        hc = acc + cb_ref[...]
        mu = jnp.mean(hc, axis=-1, keepdims=True)
        var = jnp.mean(jnp.square(hc - mu), axis=-1, keepdims=True)
        hn = (hc - mu) * lax.rsqrt(var + EPS) * lg_ref[...] + lb_ref[...]
        act = _silu(hn).astype(BF16)
        yb = _dot(act, pw_ref[...]) + pb_ref[...]
        rows = slice(r * CONV_SUB, (r + 1) * CONV_SUB)
        o_ref[rows, :] = (yb * sbg_ref[rows, :].astype(F32)).astype(BF16)


def _conformer(glu, sbg, cw, cb, lg, lb, pw, pb, batch, seq):
    tt = 512
    nt = seq // tt
    per = tt // CONV_HALO
    vec = lambda a: pl.BlockSpec(a.shape, lambda b, i: (0, 0))
    return pl.pallas_call(
        functools.partial(_conv_kernel, tt=tt),
        grid=(batch, nt),
        in_specs=[pl.BlockSpec((tt, B_WIDTH), lambda b, i: (b * nt + i, 0)),
                  pl.BlockSpec((CONV_HALO, B_WIDTH),
                               lambda b, i: (jnp.maximum((b * nt + i) * per - 1, 0), 0)),
                  pl.BlockSpec((tt, B_WIDTH), lambda b, i: (b * nt + i, 0)),
                  vec(cw), vec(cb), vec(lg), vec(lb), vec(pw), vec(pb)],
        out_specs=pl.BlockSpec((tt, B_WIDTH), lambda b, i: (b * nt + i, 0)),
        out_shape=jax.ShapeDtypeStruct((batch * seq, B_WIDTH), BF16),
        scratch_shapes=[pltpu.VMEM((CONV_HALO + tt, B_WIDTH), F32)],
        compiler_params=pltpu.CompilerParams(dimension_semantics=("arbitrary", "arbitrary")),
        name="conformer_conv",
    )(glu, glu, sbg, cw, cb, lg, lb, pw, pb)


def _gla_kernel(q_ref, k_ref, v_ref, g_ref, scg_ref, gn_ref, o_ref, st_ref, *, tg):
    c = GLA_CHUNK

    @pl.when(pl.program_id(1) == 0)
    def _():
        st_ref[...] = jnp.zeros_like(st_ref)

    ri = lax.broadcasted_iota(I32, (c, c), 0)
    ci = lax.broadcasted_iota(I32, (c, c), 1)
    tri = jnp.where(ri >= ci, 1.0, 0.0).astype(BF16)
    r2 = lax.broadcasted_iota(I32, (C_WIDTH, C_KEYS), 0)
    c2 = lax.broadcasted_iota(I32, (C_WIDTH, C_KEYS), 1)
    key_blk = (r2 // C_VAL_DIM) == (c2 // C_KEY_DIM)
    r3 = lax.broadcasted_iota(I32, (C_WIDTH, C_WIDTH), 0)
    c3 = lax.broadcasted_iota(I32, (C_WIDTH, C_WIDTH), 1)
    val_blk = (r3 // C_VAL_DIM) == (c3 // C_VAL_DIM)
    ones_blk = jnp.where(val_blk, 1.0, 0.0).astype(BF16)
    r4 = lax.broadcasted_iota(I32, (c, C_WIDTH), 0)
    c4 = lax.broadcasted_iota(I32, (c, C_WIDTH), 1)
    causal = (c4 % c) <= r4

    def chunk(n, carry):
        rows = pl.ds(pl.multiple_of(n * c, c), c)
        q = q_ref[rows, :]
        k = k_ref[rows, :]
        v = v_ref[rows, :]
        g1, g2, g3 = _split3(g_ref[rows, :])
        b = _dot(tri, g1) + _dot(tri, g2) + _dot(tri, g3)
        b_last = b[c - 1:c, :]
        qe = (q * jnp.exp(b)).astype(BF16)
        ke = k * jnp.exp(-b)
        kd = (k * jnp.exp(b_last - b)).astype(BF16)
        st = st_ref[...]

        k_bd = jnp.where(key_blk, jnp.concatenate([ke] * C_HEADS, axis=0), 0.0).astype(BF16)
        a = jnp.where(causal, _dot_nt(qe, k_bd), 0.0)
        v_bd = jnp.where(val_blk, jnp.concatenate([v] * C_HEADS, axis=0), 0.0).astype(BF16)
        o = _dot(a.astype(BF16), v_bd) + _dot_nt(qe, st.astype(BF16))

        upd = _dot(v.T.astype(BF16), kd)
        st_ref[...] = st * jnp.exp(b_last) + jnp.where(key_blk, upd, 0.0)

        oh, ol, _ = _split3(o * o)
        ms = (_dot(oh, ones_blk) + _dot(ol, ones_blk)) * (1.0 / C_VAL_DIM)
        on = o * lax.rsqrt(ms + EPS) * gn_ref[...]
        o_ref[rows, :] = (on * scg_ref[rows, :].astype(F32)).astype(BF16)
        return carry

    lax.fori_loop(0, tg // c, chunk, 0)


def _gla(cq, ck, cv, la, scg, gn, batch, seq):
    tg = 512
    nt = seq // tg
    blk = lambda w: pl.BlockSpec((tg, w), lambda b, i: (b * nt + i, 0))
    return pl.pallas_call(
        functools.partial(_gla_kernel, tg=tg),
        grid=(batch, nt),
        in_specs=[blk(C_KEYS), blk(C_KEYS), blk(C_WIDTH), blk(C_KEYS), blk(C_WIDTH),
                  pl.BlockSpec(gn.shape, lambda b, i: (0, 0))],
        out_specs=blk(C_WIDTH),
        out_shape=jax.ShapeDtypeStruct((batch * seq, C_WIDTH), BF16),
        scratch_shapes=[pltpu.VMEM((C_WIDTH, C_KEYS), F32)],
        compiler_params=pltpu.CompilerParams(dimension_semantics=("arbitrary", "arbitrary")),
        name="gla",
    )(cq, ck, cv, la, scg, gn)


def _outproj_kernel(ya_ref, yb_ref, yc_ref, x_ref, wo_ref, g_ref, b_ref, o_ref, *, alpha):
    y = (_dot(ya_ref[...], wo_ref[0:A_WIDTH, :])
         + _dot(yb_ref[...], wo_ref[A_WIDTH:A_WIDTH + B_WIDTH, :])
         + _dot(yc_ref[...], wo_ref[A_WIDTH + B_WIDTH:, :]))
    z = alpha * x_ref[...] + y
    mu = jnp.mean(z, axis=-1, keepdims=True)
    var = jnp.mean(jnp.square(z - mu), axis=-1, keepdims=True)
    o_ref[...] = (z - mu) * lax.rsqrt(var + EPS) * g_ref[...] + b_ref[...]


def _outproj(ya, yb, yc, x2, wo, g, b, alpha):
    m, d = x2.shape
    tm = 512
    row = lambda w: pl.BlockSpec((tm, w), lambda i: (i, 0))
    full = lambda a: pl.BlockSpec(a.shape, lambda i: (0, 0))
    return pl.pallas_call(
        functools.partial(_outproj_kernel, alpha=alpha),
        grid=(m // tm,),
        in_specs=[row(A_WIDTH), row(B_WIDTH), row(C_WIDTH), row(d), full(wo), full(g), full(b)],
        out_specs=row(d),
        out_shape=jax.ShapeDtypeStruct((m, d), F32),
        compiler_params=pltpu.CompilerParams(dimension_semantics=("arbitrary",)),
        name="out_proj",
    )(ya, yb, yc, x2, wo, g, b)


def _pack_in_weights(w):
    sizes = (A_WIDTH, A_WIDTH, A_WIDTH, A_WIDTH, IDX_HEADS * IDX_DIM, IDX_DIM, IDX_HEADS,
             2 * B_WIDTH, B_WIDTH, C_KEYS, C_KEYS, C_WIDTH, C_WIDTH, GATE_RANK)
    offs = np.concatenate([[0], np.cumsum(sizes)])
    (a_q, a_k, a_v, a_g, i_q, i_k, i_w, b_glu, b_g, c_q, c_k, c_v, c_g, c_lr) = [
        w[:, offs[n]:offs[n + 1]] for n in range(len(sizes))]
    d = w.shape[0]
    pad = lambda a, n: jnp.concatenate([a, jnp.zeros((d, n - a.shape[1]), a.dtype)], axis=1)
    wt = jnp.concatenate([a_q, i_q, a_v, pad(i_w, T_END - T_WI)], axis=1).T
    wr = jnp.concatenate([a_k, pad(i_k, LANES), a_g, b_glu, b_g, c_q, c_k, c_v, c_g,
                          pad(c_lr, LANES)], axis=1)
    return wt.astype(BF16), wr.astype(BF16)


def kernel(x, positions, w_in, conv_w, conv_b, cln_g, cln_b, pw_w, pw_b, gate_w2, gate_b,
           gnorm_g, w_out, ln_g, ln_b):
    batch, seq, d = x.shape
    depth = w_in.shape[0]
    alpha = float((2 * depth) ** 0.25)
    m = batch * seq
    tables = _rope_tables(positions)
    x2 = x.reshape(m, d)
    for l in range(depth):
        wt, wr = _pack_in_weights(w_in[l])
        w2 = jnp.zeros((LANES, C_KEYS), F32).at[:GATE_RANK].set(gate_w2[l])
        w2h = w2.astype(BF16)
        w2l = (w2 - w2h.astype(F32)).astype(BF16)
        (qt, qit, vt, wit, k, ki, sag, glu, sbg, cq, ck, cv, scg, la) = _inproj(
            x2, wt, wr, w2h, w2l, gate_b[l].reshape(1, C_KEYS), tables)
        ya = _attention(qt, qit, wit, sag, k, ki, vt, batch, seq)
        yb = _conformer(glu, sbg, conv_w[l], conv_b[l].reshape(1, -1), cln_g[l].reshape(1, -1),
                        cln_b[l].reshape(1, -1), pw_w[l].astype(BF16), pw_b[l].reshape(1, -1),
                        batch, seq)
        yc = _gla(cq, ck, cv, la, scg, gnorm_g[l].reshape(1, -1), batch, seq)
        x2 = _outproj(ya, yb, yc, x2, w_out[l].astype(BF16), ln_g[l].reshape(1, -1),
                      ln_b[l].reshape(1, -1), alpha)
    return x2.reshape(batch, seq, d)
```

```python
import functools

import numpy as np
import jax
import jax.numpy as jnp
from jax import lax
from jax.experimental import pallas as pl
from jax.experimental.pallas import tpu as pltpu

F32 = jnp.float32
BF16 = jnp.bfloat16
I32 = jnp.int32

A_HEADS = 8
A_HEAD_DIM = 64
A_WIDTH = A_HEADS * A_HEAD_DIM
IDX_HEADS = 8
IDX_DIM = 64
TOPK_MAX = 256
B_WIDTH = 256
CONV_WIDTH = 31
C_HEADS = 4
C_KEY_DIM = 32
C_VAL_DIM = 64
C_KEYS = C_HEADS * C_KEY_DIM
C_WIDTH = C_HEADS * C_VAL_DIM
GATE_RANK = 16
GATE_TAU = 16.0
GLA_CHUNK = 64
ROPE_THETA = 10000.0
ROPE_HALF = 32
EPS = 1e-5

LANES = 128
VMEM_LIMIT = 56 * 1024 * 1024
INT_MIN = int(np.iinfo(np.int32).min)
NEG_BIG = -1e30
Q_SCALE = A_HEAD_DIM ** -0.5 * float(np.log2(np.e))

R_K, R_KI, R_AG, R_GV, R_GG, R_BG, R_CQ, R_CK, R_CV, R_CG, R_LR, R_END = (
    0, 512, 640, 1152, 1408, 1664, 1920, 2048, 2176, 2432, 2688, 2816)
T_Q, T_QI, T_V, T_WI, T_END = 0, 512, 1024, 1536, 1552


def _dot(a, b):
    return jnp.dot(a, b, preferred_element_type=F32)


def _dot_nt(a, b):
    return lax.dot_general(a, b, (((1,), (1,)), ((), ())), preferred_element_type=F32)


def _sigmoid(x):
    return 1.0 / (1.0 + jnp.exp(-x))


def _silu(x):
    return x * _sigmoid(x)


REDUCE_WAYS = 4


def _col_partial(x, op):
    r, c = x.shape
    return op(x.reshape(r // (8 * REDUCE_WAYS), REDUCE_WAYS, 8, c), axis=0)


def _col_reduce(x, op):
    return op(op(_col_partial(x, op), axis=0), axis=0, keepdims=True)


def _split3(x):
    a = x.astype(BF16)
    r = x - a.astype(F32)
    b = r.astype(BF16)
    c = (r - b.astype(F32)).astype(BF16)
    return a, b, c


def _rope_kernel(post_ref, posc_ref, invc_ref, invr_ref, sgn_ref,
                 cost_ref, sint_ref, cosr_ref, sinr_ref):
    ang_t = invc_ref[...] * post_ref[...]
    cost_ref[...] = jnp.cos(ang_t)
    sint_ref[...] = jnp.sin(ang_t)
    ang_r = posc_ref[...] * invr_ref[...]
    cosr_ref[...] = jnp.cos(ang_r)
    sinr_ref[...] = jnp.sin(ang_r) * sgn_ref[...]


def _rope_tables(positions):
    m = positions.size
    tm = 1024
    pos = positions.reshape(-1).astype(F32)
    inv = ROPE_THETA ** (-jnp.arange(0, 2 * ROPE_HALF, 2, dtype=F32) / (2 * ROPE_HALF))
    lane = np.arange(LANES)
    sgn = jnp.asarray(np.where(lane % 64 < ROPE_HALF, -1.0, 1.0), F32).reshape(1, LANES)
    inv_r = jnp.tile(inv, LANES // ROPE_HALF).reshape(1, LANES)
    return pl.pallas_call(
        _rope_kernel,
        grid=(m // tm,),
        in_specs=[pl.BlockSpec((1, tm), lambda i: (0, i)),
                  pl.BlockSpec((tm, 1), lambda i: (i, 0)),
                  pl.BlockSpec((ROPE_HALF, 1), lambda i: (0, 0)),
                  pl.BlockSpec((1, LANES), lambda i: (0, 0)),
                  pl.BlockSpec((1, LANES), lambda i: (0, 0))],
        out_specs=[pl.BlockSpec((ROPE_HALF, tm), lambda i: (0, i)),
                   pl.BlockSpec((ROPE_HALF, tm), lambda i: (0, i)),
                   pl.BlockSpec((tm, LANES), lambda i: (i, 0)),
                   pl.BlockSpec((tm, LANES), lambda i: (i, 0))],
        out_shape=[jax.ShapeDtypeStruct((ROPE_HALF, m), F32),
                   jax.ShapeDtypeStruct((ROPE_HALF, m), F32),
                   jax.ShapeDtypeStruct((m, LANES), F32),
                   jax.ShapeDtypeStruct((m, LANES), F32)],
        name="rope_tables",
    )(pos.reshape(1, m), pos.reshape(m, 1), inv.reshape(ROPE_HALF, 1), inv_r, sgn)


def _inproj_kernel(x_ref, wt_ref, wr_ref, w2h_ref, w2l_ref, gb_ref,
                   cost_ref, sint_ref, cosr_ref, sinr_ref,
                   qt_ref, qit_ref, vt_ref, wit_ref, k_ref, ki_ref, sag_ref,
                   glu_ref, sbg_ref, cq_ref, ck_ref, cv_ref, scg_ref, la_ref):
    xb = x_ref[...].astype(BF16)
    ct = cost_ref[...]
    st = sint_ref[...]

    ht = _dot_nt(wt_ref[...], xb)
    for h in range(A_HEADS):
        for base, out, scale in ((T_Q, qt_ref, Q_SCALE), (T_QI, qit_ref, 1.0)):
            r0 = base + h * 64
            x1 = ht[r0:r0 + 32]
            x2 = ht[r0 + 32:r0 + 64]
            o0 = h * 64
            out[o0:o0 + 32, :] = ((x1 * ct - x2 * st) * scale).astype(BF16)
            out[o0 + 32:o0 + 64, :] = ((x2 * ct + x1 * st) * scale).astype(BF16)
    vt_ref[...] = ht[T_V:T_V + A_WIDTH].astype(BF16)
    wit_ref[...] = ht[T_WI:T_WI + IDX_HEADS] * (IDX_HEADS ** -0.5 * IDX_DIM ** -0.5)

    cr = cosr_ref[...]
    sr = sinr_ref[...]
    lane = lax.broadcasted_iota(I32, cr.shape, 1)
    first_half = (lane % 64) < ROPE_HALF

    def rope_rows(v):
        fwd = pltpu.roll(v, LANES - ROPE_HALF, 1)
        bwd = pltpu.roll(v, ROPE_HALF, 1)
        return v * cr + jnp.where(first_half, fwd, bwd) * sr

    for g in range(A_WIDTH // LANES):
        c0 = R_K + g * LANES
        hk = _dot(xb, wr_ref[:, c0:c0 + LANES])
        k_ref[:, g * LANES:(g + 1) * LANES] = rope_rows(hk).astype(BF16)
    hki = _dot(xb, wr_ref[:, R_KI:R_KI + LANES])
    ki_ref[...] = rope_rows(hki)[:, :IDX_DIM].astype(BF16)

    sag_ref[...] = _silu(_dot(xb, wr_ref[:, R_AG:R_AG + A_WIDTH])).astype(BF16)
    gv = _dot(xb, wr_ref[:, R_GV:R_GV + B_WIDTH])
    gg = _dot(xb, wr_ref[:, R_GG:R_GG + B_WIDTH])
    glu_ref[...] = gv * _sigmoid(gg)
    sbg_ref[...] = _silu(_dot(xb, wr_ref[:, R_BG:R_BG + B_WIDTH])).astype(BF16)
    cq_ref[...] = _dot(xb, wr_ref[:, R_CQ:R_CQ + C_KEYS]) * (C_KEY_DIM ** -0.5)
    ck_ref[...] = _dot(xb, wr_ref[:, R_CK:R_CK + C_KEYS])
    cv_ref[...] = _dot(xb, wr_ref[:, R_CV:R_CV + C_WIDTH])
    scg_ref[...] = _silu(_dot(xb, wr_ref[:, R_CG:R_CG + C_WIDTH])).astype(BF16)

    lr = _dot(xb, wr_ref[:, R_LR:R_LR + LANES])
    lr_h = lr.astype(BF16)
    lr_l = (lr - lr_h.astype(F32)).astype(BF16)
    z = (_dot(lr_h, w2h_ref[...]) + _dot(lr_l, w2h_ref[...]) + _dot(lr_h, w2l_ref[...])
         + gb_ref[...])
    log_sig = jnp.minimum(z, 0.0) - jnp.log(1.0 + jnp.exp(-jnp.abs(z)))
    la_ref[...] = log_sig * (1.0 / GATE_TAU)


def _inproj(x2, wt, wr, w2h, w2l, gb, tables):
    m, d = x2.shape
    tm = 512
    cost, sint, cosr, sinr = tables
    row = lambda w: pl.BlockSpec((tm, w), lambda i: (i, 0))
    col = lambda r: pl.BlockSpec((r, tm), lambda i: (0, i))
    full = lambda a: pl.BlockSpec(a.shape, lambda i: (0, 0))
    outs = [(col(A_WIDTH), (A_WIDTH, m), BF16),
            (col(A_WIDTH), (A_WIDTH, m), BF16),
            (col(A_WIDTH), (A_WIDTH, m), BF16),
            (col(IDX_HEADS), (IDX_HEADS, m), F32),
            (row(A_WIDTH), (m, A_WIDTH), BF16),
            (row(IDX_DIM), (m, IDX_DIM), BF16),
            (row(A_WIDTH), (m, A_WIDTH), BF16),
            (row(B_WIDTH), (m, B_WIDTH), F32),
            (row(B_WIDTH), (m, B_WIDTH), BF16),
            (row(C_KEYS), (m, C_KEYS), F32),
            (row(C_KEYS), (m, C_KEYS), F32),
            (row(C_WIDTH), (m, C_WIDTH), F32),
            (row(C_WIDTH), (m, C_WIDTH), BF16),
            (row(C_KEYS), (m, C_KEYS), F32)]
    return pl.pallas_call(
        _inproj_kernel,
        grid=(m // tm,),
        in_specs=[row(d), full(wt), full(wr), full(w2h), full(w2l), full(gb),
                  col(ROPE_HALF), col(ROPE_HALF), row(LANES), row(LANES)],
        out_specs=[o[0] for o in outs],
        out_shape=[jax.ShapeDtypeStruct(o[1], o[2]) for o in outs],
        compiler_params=pltpu.CompilerParams(
            dimension_semantics=("arbitrary",), vmem_limit_bytes=VMEM_LIMIT),
        name="in_proj",
    )(x2, wt, wr, w2h, w2l, gb, cost, sint, cosr, sinr)


KEY_LOWEST_FINITE = -2139095040
ATTN_TQ = 256
ATTN_TK = 256


def _key_to_float(key):
    return pltpu.bitcast(key ^ ((key >> 31) & 0x7FFFFFFF), F32)


def _attn_kernel(qt_ref, qit_ref, wit_ref, sag_ref, k_ref, ki_ref, vt_ref,
                 o_ref, sc_ref, hi_ref, qm_ref, acc_ref, cut_ref, s_ref, p_ref, *, tq, tk, topk, seq):
    i = pl.program_id(1)
    nchunks = (i + 1) * (tq // tk)
    rel = (lax.broadcasted_iota(I32, (tk, tq), 0) - lax.broadcasted_iota(I32, (tk, tq), 1))

    def score_chunk(j, carry):
        ks = pl.multiple_of(j * tk, tk)
        kic = ki_ref[pl.ds(ks, tk), :]
        sc = jnp.zeros((tk, tq), F32)
        for h in range(IDX_HEADS):
            lg = _dot(kic, qit_ref[h * IDX_DIM:(h + 1) * IDX_DIM, :])
            sc = sc + wit_ref[h:h + 1, :] * jnp.maximum(lg, 0.0)
        sc = jnp.where(rel <= i * tq - j * tk, sc, -jnp.inf)
        sc_ref[pl.ds(ks, tk), :] = sc
        upper = pltpu.bitcast(pltpu.bitcast(sc, I32) & -65536, F32)
        hi_ref[pl.ds(ks, tk), :] = upper.astype(BF16)
        return carry

    lax.fori_loop(0, nchunks, score_chunk, 0)

    packed_rows = 16
    one16 = jnp.ones((tk, tq), BF16)
    zero16 = jnp.zeros((tk, tq), BF16)

    def count_upper(cand):
        def body(j, acc):
            ks = pl.multiple_of(j * tk, tk)
            hit = jnp.where(hi_ref[pl.ds(ks, tk), :] >= cand, one16, zero16)
            hit = hit.reshape(tk // (packed_rows * REDUCE_WAYS), REDUCE_WAYS, packed_rows, tq)
            part = hit[0]
            for r in range(1, hit.shape[0]):
                part = part + hit[r]
            return acc + part.astype(F32)
        acc = lax.fori_loop(0, nchunks, body, jnp.zeros((REDUCE_WAYS, packed_rows, tq), F32))
        return acc.sum(axis=0).sum(axis=0, keepdims=True).astype(I32)

    def upper_key_to_bf16(key16):
        pattern = (key16 ^ ((key16 >> 31) & 0x7FFF)) & 0xFFFF
        return pltpu.bitcast(jnp.left_shift(pattern, 16), F32).astype(BF16)

    def count(hits):
        def body(j, acc):
            ks = pl.multiple_of(j * tk, tk)
            return acc + _col_partial(hits(sc_ref[pl.ds(ks, tk), :], ks), jnp.sum)
        acc = lax.fori_loop(0, nchunks, body, jnp.zeros((REDUCE_WAYS, 8, tq), I32))
        return acc.sum(axis=0).sum(axis=0, keepdims=True)

    def count_ge(thr):
        return count(lambda s, ks: jnp.where(s >= thr, 1, 0))

    zero = jnp.zeros((1, tq), I32)
    cnt0 = count_upper(jnp.zeros((1, tq), BF16))
    tau = jnp.where(cnt0 >= topk, zero, -32768)
    cnt_tau = jnp.where(cnt0 >= topk, cnt0, zero)

    def upper_step(it, carry):
        tau, cnt_tau = carry
        cand = tau + jnp.left_shift(jnp.int32(1), 14 - it)
        cnt = count_upper(upper_key_to_bf16(cand))
        take = cnt >= topk
        return jnp.where(take, cand, tau), jnp.where(take, cnt, cnt_tau)

    tau, cnt_tau = lax.fori_loop(0, 15, upper_step, (tau, cnt_tau))
    tau = jnp.left_shift(tau, 16)

    def lower_step(it, carry):
        tau, cnt_tau = carry
        cand = tau + jnp.left_shift(jnp.int32(1), 15 - it)
        cnt = count_ge(_key_to_float(cand))
        take = cnt >= topk
        return jnp.where(take, cand, tau), jnp.where(take, cnt, cnt_tau)

    tau, cnt_tau = lax.fori_loop(0, 16, lower_step, (tau, cnt_tau))
    thr = _key_to_float(jnp.maximum(tau, KEY_LOWEST_FINITE))

    has_ties = jnp.max(cnt_tau) > topk

    @pl.when(jnp.logical_not(has_ties))
    def _():
        def bias_chunk(j, carry):
            ks = pl.multiple_of(j * tk, tk)
            sc_ref[pl.ds(ks, tk), :] = jnp.where(sc_ref[pl.ds(ks, tk), :] >= thr, 0.0, NEG_BIG)
            return carry

        lax.fori_loop(0, nchunks, bias_chunk, 0)

    @pl.when(has_ties)
    def _():
        need = topk - count(lambda s, ks: jnp.where(s > thr, 1, 0))
        row = lax.broadcasted_iota(I32, (tk, tq), 0)
        nbits = (seq - 1).bit_length()

        def idx_step(it, x):
            cand = x + jnp.left_shift(jnp.int32(1), nbits - 1 - it)
            below = count(lambda s, ks: jnp.where(s == thr, jnp.where(row + ks < cand, 1, 0), 0))
            return jnp.where(below < need, cand, x)

        cut_ref[...] = lax.fori_loop(0, nbits, idx_step, zero)

        def bias_chunk(j, carry):
            ks = pl.multiple_of(j * tk, tk)
            s = sc_ref[pl.ds(ks, tk), :]
            tied = jnp.where(row + ks <= cut_ref[...], 0.0, NEG_BIG)
            sc_ref[pl.ds(ks, tk), :] = jnp.where(s > thr, 0.0, jnp.where(s == thr, tied, NEG_BIG))
            return carry

        lax.fori_loop(0, nchunks, bias_chunk, 0)

    head_row = lax.broadcasted_iota(I32, (LANES, tq), 0) // A_HEAD_DIM
    for h in range(A_HEADS):
        qpair = qt_ref[(h // 2) * LANES:(h // 2 + 1) * LANES, :]
        qm_ref[h] = jnp.where(head_row == (h % 2), qpair, jnp.zeros_like(qpair))
    acc_ref[...] = jnp.zeros(acc_ref.shape, F32)

    def add_values(j, slot, alpha):
        ks = pl.multiple_of(j * tk, tk)
        for h in range(A_HEADS):
            hs = slice(h * A_HEAD_DIM, (h + 1) * A_HEAD_DIM)
            pv = _dot(vt_ref[hs, pl.ds(ks, tk)], p_ref[slot * A_HEADS + h])
            acc_ref[hs, :] = alpha[h:h + 1, :] * acc_ref[hs, :] + pv

    def attn_chunk(j, carry):
        m_run, l_run, alpha_prev = carry
        slot = j % 2
        add_values(jnp.maximum(j - 1, 0), 1 - slot, alpha_prev)
        ks = pl.multiple_of(j * tk, tk)
        tops = []
        for h in range(A_HEADS):
            kc = k_ref[pl.ds(ks, tk), (h // 2) * LANES:(h // 2 + 1) * LANES]
            s = _dot(kc, qm_ref[h]) + sc_ref[pl.ds(ks, tk), :]
            s_ref[slot * A_HEADS + h] = s
            tops.append(_col_reduce(s, jnp.max))
        m_new = jnp.maximum(m_run, jnp.concatenate(tops, axis=0))
        alpha = jnp.exp2(m_run - m_new)
        sums = []
        for h in range(A_HEADS):
            p = jnp.exp2(s_ref[slot * A_HEADS + h] - m_new[h:h + 1, :])
            p_ref[slot * A_HEADS + h] = p.astype(BF16)
            sums.append(_col_reduce(p, jnp.sum))
        l_new = alpha * l_run + jnp.concatenate(sums, axis=0)
        return m_new, l_new, alpha

    p_ref[A_HEADS:2 * A_HEADS] = jnp.zeros((A_HEADS, tk, tq), BF16)
    init = (jnp.full((A_HEADS, tq), NEG_BIG, F32), jnp.zeros((A_HEADS, tq), F32),
            jnp.ones((A_HEADS, tq), F32))
    _, l_fin, alpha_last = lax.fori_loop(0, nchunks, attn_chunk, init)
    add_values(nchunks - 1, (nchunks - 1) % 2, alpha_last)

    for h in range(A_HEADS):
        hs = slice(h * A_HEAD_DIM, (h + 1) * A_HEAD_DIM)
        acc_ref[hs, :] = acc_ref[hs, :] / l_fin[h:h + 1, :]
    o_ref[...] = (acc_ref[...].T * sag_ref[...].astype(F32)).astype(BF16)


def _attention(qt, qit, wit, sag, k, ki, vt, batch, seq):
    tq, tk = ATTN_TQ, ATTN_TK
    nq = seq // tq
    topk = min(TOPK_MAX, seq // 4)
    once = pl.Buffered(1)
    return pl.pallas_call(
        functools.partial(_attn_kernel, tq=tq, tk=tk, topk=topk, seq=seq),
        grid=(batch, nq),
        in_specs=[pl.BlockSpec((A_WIDTH, tq), lambda b, i: (0, b * nq + i)),
                  pl.BlockSpec((A_WIDTH, tq), lambda b, i: (0, b * nq + i)),
                  pl.BlockSpec((IDX_HEADS, tq), lambda b, i: (0, b * nq + i)),
                  pl.BlockSpec((tq, A_WIDTH), lambda b, i: (b * nq + i, 0)),
                  pl.BlockSpec((seq, A_WIDTH), lambda b, i: (b, 0), pipeline_mode=once),
                  pl.BlockSpec((seq, IDX_DIM), lambda b, i: (b, 0), pipeline_mode=once),
                  pl.BlockSpec((A_WIDTH, seq), lambda b, i: (0, b), pipeline_mode=once)],
        out_specs=pl.BlockSpec((tq, A_WIDTH), lambda b, i: (b * nq + i, 0)),
        out_shape=jax.ShapeDtypeStruct((batch * seq, A_WIDTH), BF16),
        scratch_shapes=[pltpu.VMEM((seq, tq), F32),
                        pltpu.VMEM((seq, tq), BF16),
                        pltpu.VMEM((A_HEADS, LANES, tq), BF16),
                        pltpu.VMEM((A_WIDTH, tq), F32),
                        pltpu.VMEM((1, tq), I32),
                        pltpu.VMEM((2 * A_HEADS, tk, tq), F32),
                        pltpu.VMEM((2 * A_HEADS, tk, tq), BF16)],
        compiler_params=pltpu.CompilerParams(
            dimension_semantics=("arbitrary", "arbitrary"), vmem_limit_bytes=VMEM_LIMIT),
        name="dsa_attention",
    )(qt, qit, wit, sag, k, ki, vt)


CONV_HALO = 32
CONV_SUB = 64


def _conv_kernel(cur_ref, halo_ref, sbg_ref, cw_ref, cb_ref, lg_ref, lb_ref, pw_ref, pb_ref,
                 o_ref, buf_ref, *, tt):
    i = pl.program_id(1)
    halo = halo_ref[...]
    buf_ref[0:CONV_HALO, :] = jnp.where(i == 0, jnp.zeros_like(halo), halo)
    buf_ref[CONV_HALO:CONV_HALO + tt, :] = cur_ref[...]
    lead = CONV_HALO - (CONV_WIDTH - 1)
    for r in range(tt // CONV_SUB):
        acc = jnp.zeros((CONV_SUB, B_WIDTH), F32)
        for j in range(CONV_WIDTH):
            start = r * CONV_SUB + lead + j
            acc = acc + buf_ref[start:start + CONV_SUB, :] * cw_ref[j:j + 1, :]
        hc = acc + cb_ref[...]
        mu = jnp.mean(hc, axis=-1, keepdims=True)
        var = jnp.mean(jnp.square(hc - mu), axis=-1, keepdims=True)
        hn = (hc - mu) * lax.rsqrt(var + EPS) * lg_ref[...] + lb_ref[...]
        act = _silu(hn).astype(BF16)
        yb = _dot(act, pw_ref[...]) + pb_ref[...]
        rows = slice(r * CONV_SUB, (r + 1) * CONV_SUB)
        o_ref[rows, :] = (yb * sbg_ref[rows, :].astype(F32)).astype(BF16)


def _conformer(glu, sbg, cw, cb, lg, lb, pw, pb, batch, seq):
    tt = 512
    nt = seq // tt
    per = tt // CONV_HALO
    vec = lambda a: pl.BlockSpec(a.shape, lambda b, i: (0, 0))
    return pl.pallas_call(
        functools.partial(_conv_kernel, tt=tt),
        grid=(batch, nt),
        in_specs=[pl.BlockSpec((tt, B_WIDTH), lambda b, i: (b * nt + i, 0)),
                  pl.BlockSpec((CONV_HALO, B_WIDTH),
                               lambda b, i: (jnp.maximum((b * nt + i) * per - 1, 0), 0)),
                  pl.BlockSpec((tt, B_WIDTH), lambda b, i: (b * nt + i, 0)),
                  vec(cw), vec(cb), vec(lg), vec(lb), vec(pw), vec(pb)],
        out_specs=pl.BlockSpec((tt, B_WIDTH), lambda b, i: (b * nt + i, 0)),
        out_shape=jax.ShapeDtypeStruct((batch * seq, B_WIDTH), BF16),
        scratch_shapes=[pltpu.VMEM((CONV_HALO + tt, B_WIDTH), F32)],
        compiler_params=pltpu.CompilerParams(dimension_semantics=("arbitrary", "arbitrary")),
        name="conformer_conv",
    )(glu, glu, sbg, cw, cb, lg, lb, pw, pb)


def _gla_kernel(q_ref, k_ref, v_ref, g_ref, scg_ref, gn_ref, o_ref, st_ref, *, tg):
    c = GLA_CHUNK

    @pl.when(pl.program_id(1) == 0)
    def _():
        st_ref[...] = jnp.zeros_like(st_ref)

    ri = lax.broadcasted_iota(I32, (c, c), 0)
    ci = lax.broadcasted_iota(I32, (c, c), 1)
    tri = jnp.where(ri >= ci, 1.0, 0.0).astype(BF16)
    r2 = lax.broadcasted_iota(I32, (C_WIDTH, C_KEYS), 0)
    c2 = lax.broadcasted_iota(I32, (C_WIDTH, C_KEYS), 1)
    key_blk = (r2 // C_VAL_DIM) == (c2 // C_KEY_DIM)
    r3 = lax.broadcasted_iota(I32, (C_WIDTH, C_WIDTH), 0)
    c3 = lax.broadcasted_iota(I32, (C_WIDTH, C_WIDTH), 1)
    val_blk = (r3 // C_VAL_DIM) == (c3 // C_VAL_DIM)
    ones_blk = jnp.where(val_blk, 1.0, 0.0).astype(BF16)
    r4 = lax.broadcasted_iota(I32, (c, C_WIDTH), 0)
    c4 = lax.broadcasted_iota(I32, (c, C_WIDTH), 1)
    causal = (c4 % c) <= r4

    def chunk(n, carry):
        rows = pl.ds(pl.multiple_of(n * c, c), c)
        q = q_ref[rows, :]
        k = k_ref[rows, :]
        v = v_ref[rows, :]
        g1, g2, g3 = _split3(g_ref[rows, :])
        b = _dot(tri, g1) + _dot(tri, g2) + _dot(tri, g3)
        b_last = b[c - 1:c, :]
        qe = (q * jnp.exp(b)).astype(BF16)
        ke = k * jnp.exp(-b)
        kd = (k * jnp.exp(b_last - b)).astype(BF16)
        st = st_ref[...]

        k_bd = jnp.where(key_blk, jnp.concatenate([ke] * C_HEADS, axis=0), 0.0).astype(BF16)
        a = jnp.where(causal, _dot_nt(qe, k_bd), 0.0)
        v_bd = jnp.where(val_blk, jnp.concatenate([v] * C_HEADS, axis=0), 0.0).astype(BF16)
        o = _dot(a.astype(BF16), v_bd) + _dot_nt(qe, st.astype(BF16))

        upd = _dot(v.T.astype(BF16), kd)
        st_ref[...] = st * jnp.exp(b_last) + jnp.where(key_blk, upd, 0.0)

        oh, ol, _ = _split3(o * o)
        ms = (_dot(oh, ones_blk) + _dot(ol, ones_blk)) * (1.0 / C_VAL_DIM)
        on = o * lax.rsqrt(ms + EPS) * gn_ref[...]
        o_ref[rows, :] = (on * scg_ref[rows, :].astype(F32)).astype(BF16)
        return carry

    lax.fori_loop(0, tg // c, chunk, 0)


def _gla(cq, ck, cv, la, scg, gn, batch, seq):
    tg = 512
    nt = seq // tg
    blk = lambda w: pl.BlockSpec((tg, w), lambda b, i: (b * nt + i, 0))
    return pl.pallas_call(
        functools.partial(_gla_kernel, tg=tg),
        grid=(batch, nt),
        in_specs=[blk(C_KEYS), blk(C_KEYS), blk(C_WIDTH), blk(C_KEYS), blk(C_WIDTH),
                  pl.BlockSpec(gn.shape, lambda b, i: (0, 0))],
        out_specs=blk(C_WIDTH),
        out_shape=jax.ShapeDtypeStruct((batch * seq, C_WIDTH), BF16),
        scratch_shapes=[pltpu.VMEM((C_WIDTH, C_KEYS), F32)],
        compiler_params=pltpu.CompilerParams(dimension_semantics=("arbitrary", "arbitrary")),
        name="gla",
    )(cq, ck, cv, la, scg, gn)


def _outproj_kernel(ya_ref, yb_ref, yc_ref, x_ref, wo_ref, g_ref, b_ref, o_ref, *, alpha):
    y = (_dot(ya_ref[...], wo_ref[0:A_WIDTH, :])
         + _dot(yb_ref[...], wo_ref[A_WIDTH:A_WIDTH + B_WIDTH, :])
         + _dot(yc_ref[...], wo_ref[A_WIDTH + B_WIDTH:, :]))
    z = alpha * x_ref[...] + y
    mu = jnp.mean(z, axis=-1, keepdims=True)
    var = jnp.mean(jnp.square(z - mu), axis=-1, keepdims=True)
    o_ref[...] = (z - mu) * lax.rsqrt(var + EPS) * g_ref[...] + b_ref[...]


def _outproj(ya, yb, yc, x2, wo, g, b, alpha):
    m, d = x2.shape
    tm = 512
    row = lambda w: pl.BlockSpec((tm, w), lambda i: (i, 0))
    full = lambda a: pl.BlockSpec(a.shape, lambda i: (0, 0))
    return pl.pallas_call(
        functools.partial(_outproj_kernel, alpha=alpha),
        grid=(m // tm,),
        in_specs=[row(A_WIDTH), row(B_WIDTH), row(C_WIDTH), row(d), full(wo), full(g), full(b)],
        out_specs=row(d),
        out_shape=jax.ShapeDtypeStruct((m, d), F32),
        compiler_params=pltpu.CompilerParams(dimension_semantics=("arbitrary",)),
        name="out_proj",
    )(ya, yb, yc, x2, wo, g, b)


def _pack_in_weights(w):
    sizes = (A_WIDTH, A_WIDTH, A_WIDTH, A_WIDTH, IDX_HEADS * IDX_DIM, IDX_DIM, IDX_HEADS,
             2 * B_WIDTH, B_WIDTH, C_KEYS, C_KEYS, C_WIDTH, C_WIDTH, GATE_RANK)
    offs = np.concatenate([[0], np.cumsum(sizes)])
    (a_q, a_k, a_v, a_g, i_q, i_k, i_w, b_glu, b_g, c_q, c_k, c_v, c_g, c_lr) = [
        w[:, offs[n]:offs[n + 1]] for n in range(len(sizes))]
    d = w.shape[0]
    pad = lambda a, n: jnp.concatenate([a, jnp.zeros((d, n - a.shape[1]), a.dtype)], axis=1)
    wt = jnp.concatenate([a_q, i_q, a_v, pad(i_w, T_END - T_WI)], axis=1).T
    wr = jnp.concatenate([a_k, pad(i_k, LANES), a_g, b_glu, b_g, c_q, c_k, c_v, c_g,
                          pad(c_lr, LANES)], axis=1)
    return wt.astype(BF16), wr.astype(BF16)


def kernel(x, positions, w_in, conv_w, conv_b, cln_g, cln_b, pw_w, pw_b, gate_w2, gate_b,
           gnorm_g, w_out, ln_g, ln_b):
    batch, seq, d = x.shape
    depth = w_in.shape[0]
    alpha = float((2 * depth) ** 0.25)
    m = batch * seq
    tables = _rope_tables(positions)
    x2 = x.reshape(m, d)
    for l in range(depth):
        wt, wr = _pack_in_weights(w_in[l])
        w2 = jnp.zeros((LANES, C_KEYS), F32).at[:GATE_RANK].set(gate_w2[l])
        w2h = w2.astype(BF16)
        w2l = (w2 - w2h.astype(F32)).astype(BF16)
        (qt, qit, vt, wit, k, ki, sag, glu, sbg, cq, ck, cv, scg, la) = _inproj(
            x2, wt, wr, w2h, w2l, gate_b[l].reshape(1, C_KEYS), tables)
        ya = _attention(qt, qit, wit, sag, k, ki, vt, batch, seq)
        yb = _conformer(glu, sbg, conv_w[l], conv_b[l].reshape(1, -1), cln_g[l].reshape(1, -1),
                        cln_b[l].reshape(1, -1), pw_w[l].astype(BF16), pw_b[l].reshape(1, -1),
                        batch, seq)
        yc = _gla(cq, ck, cv, la, scg, gnorm_g[l].reshape(1, -1), batch, seq)
        x2 = _outproj(ya, yb, yc, x2, w_out[l].astype(BF16), ln_g[l].reshape(1, -1),
                      ln_b[l].reshape(1, -1), alpha)
    return x2.reshape(batch, seq, d)
```

```python
import functools

import numpy as np
import jax
import jax.numpy as jnp
from jax import lax
from jax.experimental import pallas as pl
from jax.experimental.pallas import tpu as pltpu

F32 = jnp.float32
BF16 = jnp.bfloat16
I32 = jnp.int32

A_HEADS = 8
A_HEAD_DIM = 64
A_WIDTH = A_HEADS * A_HEAD_DIM
IDX_HEADS = 8
IDX_DIM = 64
TOPK_MAX = 256
B_WIDTH = 256
CONV_WIDTH = 31
C_HEADS = 4
C_KEY_DIM = 32
C_VAL_DIM = 64
C_KEYS = C_HEADS * C_KEY_DIM
C_WIDTH = C_HEADS * C_VAL_DIM
GATE_RANK = 16
GATE_TAU = 16.0
GLA_CHUNK = 64
GLA_SUB = 16
GLA_EXP_CLAMP = 80.0
ROPE_THETA = 10000.0
ROPE_HALF = 32
EPS = 1e-5

LANES = 128
SUBLANES = 8
VMEM_LIMIT = 56 * 1024 * 1024
INT_MIN = int(np.iinfo(np.int32).min)
NEG_BIG = -1e30
Q_SCALE = A_HEAD_DIM ** -0.5 * float(np.log2(np.e))

R_K, R_KI, R_AG, R_GV, R_GG, R_BG, R_CQ, R_CK, R_CV, R_CG, R_LR, R_END = (
    0, 512, 640, 1152, 1408, 1664, 1920, 2048, 2176, 2432, 2688, 2816)
T_Q, T_QI, T_V, T_WI, T_END = 0, 512, 1024, 1536, 1552


def _dot(a, b):
    return jnp.dot(a, b, preferred_element_type=F32)


def _dot_nt(a, b):
    return lax.dot_general(a, b, (((1,), (1,)), ((), ())), preferred_element_type=F32)


def _sigmoid(x):
    return 1.0 / (1.0 + jnp.exp(-x))


def _silu(x):
    return x * _sigmoid(x)


REDUCE_WAYS = 4


def _col_partial(x, op):
    r, c = x.shape
    return op(x.reshape(r // (8 * REDUCE_WAYS), REDUCE_WAYS, 8, c), axis=0)


def _col_reduce(x, op):
    return op(op(_col_partial(x, op), axis=0), axis=0, keepdims=True)


def _split3(x):
    a = x.astype(BF16)
    r = x - a.astype(F32)
    b = r.astype(BF16)
    c = (r - b.astype(F32)).astype(BF16)
    return a, b, c


def _rope_kernel(post_ref, posc_ref, invc_ref, invr_ref, sgn_ref,
                 cost_ref, sint_ref, cosr_ref, sinr_ref):
    ang_t = invc_ref[...] * post_ref[...]
    cost_ref[...] = jnp.cos(ang_t)
    sint_ref[...] = jnp.sin(ang_t)
    ang_r = posc_ref[...] * invr_ref[...]
    cosr_ref[...] = jnp.cos(ang_r)
    sinr_ref[...] = jnp.sin(ang_r) * sgn_ref[...]


def _rope_tables(positions):
    m = positions.size
    tm = 1024
    pos = positions.reshape(-1).astype(F32)
    inv = ROPE_THETA ** (-jnp.arange(0, 2 * ROPE_HALF, 2, dtype=F32) / (2 * ROPE_HALF))
    lane = np.arange(LANES)
    sgn = jnp.asarray(np.where(lane % 64 < ROPE_HALF, -1.0, 1.0), F32).reshape(1, LANES)
    inv_r = jnp.tile(inv, LANES // ROPE_HALF).reshape(1, LANES)
    return pl.pallas_call(
        _rope_kernel,
        grid=(m // tm,),
        in_specs=[pl.BlockSpec((1, tm), lambda i: (0, i)),
                  pl.BlockSpec((tm, 1), lambda i: (i, 0)),
                  pl.BlockSpec((ROPE_HALF, 1), lambda i: (0, 0)),
                  pl.BlockSpec((1, LANES), lambda i: (0, 0)),
                  pl.BlockSpec((1, LANES), lambda i: (0, 0))],
        out_specs=[pl.BlockSpec((ROPE_HALF, tm), lambda i: (0, i)),
                   pl.BlockSpec((ROPE_HALF, tm), lambda i: (0, i)),
                   pl.BlockSpec((tm, LANES), lambda i: (i, 0)),
                   pl.BlockSpec((tm, LANES), lambda i: (i, 0))],
        out_shape=[jax.ShapeDtypeStruct((ROPE_HALF, m), F32),
                   jax.ShapeDtypeStruct((ROPE_HALF, m), F32),
                   jax.ShapeDtypeStruct((m, LANES), F32),
                   jax.ShapeDtypeStruct((m, LANES), F32)],
        name="rope_tables",
    )(pos.reshape(1, m), pos.reshape(m, 1), inv.reshape(ROPE_HALF, 1), inv_r, sgn)


def _inproj_kernel(x_ref, wt_ref, wr_ref, w2h_ref, w2l_ref, gb_ref,
                   cost_ref, sint_ref, cosr_ref, sinr_ref,
                   qt_ref, qit_ref, vt_ref, wit_ref, k_ref, ki_ref, sag_ref,
                   glu_ref, sbg_ref, cq_ref, ck_ref, cv_ref, scg_ref, la_ref):
    xb = x_ref[...].astype(BF16)
    ct = cost_ref[...]
    st = sint_ref[...]

    ht = _dot_nt(wt_ref[...], xb)
    for h in range(A_HEADS):
        for base, out, scale in ((T_Q, qt_ref, Q_SCALE), (T_QI, qit_ref, 1.0)):
            r0 = base + h * 64
            x1 = ht[r0:r0 + 32]
            x2 = ht[r0 + 32:r0 + 64]
            o0 = h * 64
            out[o0:o0 + 32, :] = ((x1 * ct - x2 * st) * scale).astype(BF16)
            out[o0 + 32:o0 + 64, :] = ((x2 * ct + x1 * st) * scale).astype(BF16)
    vt_ref[...] = ht[T_V:T_V + A_WIDTH].astype(BF16)
    wit_ref[...] = ht[T_WI:T_WI + IDX_HEADS] * (IDX_HEADS ** -0.5 * IDX_DIM ** -0.5)

    cr = cosr_ref[...]
    sr = sinr_ref[...]
    lane = lax.broadcasted_iota(I32, cr.shape, 1)
    first_half = (lane % 64) < ROPE_HALF

    def rope_rows(v):
        fwd = pltpu.roll(v, LANES - ROPE_HALF, 1)
        bwd = pltpu.roll(v, ROPE_HALF, 1)
        return v * cr + jnp.where(first_half, fwd, bwd) * sr

    for g in range(A_WIDTH // LANES):
        c0 = R_K + g * LANES
        hk = _dot(xb, wr_ref[:, c0:c0 + LANES])
        k_ref[:, g * LANES:(g + 1) * LANES] = rope_rows(hk).astype(BF16)
    hki = _dot(xb, wr_ref[:, R_KI:R_KI + LANES])
    ki_ref[...] = rope_rows(hki)[:, :IDX_DIM].astype(BF16)

    sag_ref[...] = _silu(_dot(xb, wr_ref[:, R_AG:R_AG + A_WIDTH])).astype(BF16)
    gv = _dot(xb, wr_ref[:, R_GV:R_GV + B_WIDTH])
    gg = _dot(xb, wr_ref[:, R_GG:R_GG + B_WIDTH])
    glu_ref[...] = gv * _sigmoid(gg)
    sbg_ref[...] = _silu(_dot(xb, wr_ref[:, R_BG:R_BG + B_WIDTH])).astype(BF16)
    cq_ref[...] = _dot(xb, wr_ref[:, R_CQ:R_CQ + C_KEYS]) * (C_KEY_DIM ** -0.5)
    ck_ref[...] = _dot(xb, wr_ref[:, R_CK:R_CK + C_KEYS])
    cv_ref[...] = _dot(xb, wr_ref[:, R_CV:R_CV + C_WIDTH])
    scg_ref[...] = _silu(_dot(xb, wr_ref[:, R_CG:R_CG + C_WIDTH])).astype(BF16)

    lr = _dot(xb, wr_ref[:, R_LR:R_LR + LANES])
    lr_h = lr.astype(BF16)
    lr_l = (lr - lr_h.astype(F32)).astype(BF16)
    z = (_dot(lr_h, w2h_ref[...]) + _dot(lr_l, w2h_ref[...]) + _dot(lr_h, w2l_ref[...])
         + gb_ref[...])
    log_sig = jnp.minimum(z, 0.0) - jnp.log(1.0 + jnp.exp(-jnp.abs(z)))
    la_ref[...] = log_sig * (1.0 / GATE_TAU)


def _inproj(x2, wt, wr, w2h, w2l, gb, tables):
    m, d = x2.shape
    tm = 512
    cost, sint, cosr, sinr = tables
    row = lambda w: pl.BlockSpec((tm, w), lambda i: (i, 0))
    col = lambda r: pl.BlockSpec((r, tm), lambda i: (0, i))
    full = lambda a: pl.BlockSpec(a.shape, lambda i: (0, 0))
    outs = [(col(A_WIDTH), (A_WIDTH, m), BF16),
            (col(A_WIDTH), (A_WIDTH, m), BF16),
            (col(A_WIDTH), (A_WIDTH, m), BF16),
            (col(IDX_HEADS), (IDX_HEADS, m), F32),
            (row(A_WIDTH), (m, A_WIDTH), BF16),
            (row(IDX_DIM), (m, IDX_DIM), BF16),
            (row(A_WIDTH), (m, A_WIDTH), BF16),
            (row(B_WIDTH), (m, B_WIDTH), F32),
            (row(B_WIDTH), (m, B_WIDTH), BF16),
            (row(C_KEYS), (m, C_KEYS), F32),
            (row(C_KEYS), (m, C_KEYS), F32),
            (row(C_WIDTH), (m, C_WIDTH), F32),
            (row(C_WIDTH), (m, C_WIDTH), BF16),
            (row(C_KEYS), (m, C_KEYS), F32)]
    return pl.pallas_call(
        _inproj_kernel,
        grid=(m // tm,),
        in_specs=[row(d), full(wt), full(wr), full(w2h), full(w2l), full(gb),
                  col(ROPE_HALF), col(ROPE_HALF), row(LANES), row(LANES)],
        out_specs=[o[0] for o in outs],
        out_shape=[jax.ShapeDtypeStruct(o[1], o[2]) for o in outs],
        compiler_params=pltpu.CompilerParams(
            dimension_semantics=("arbitrary",), vmem_limit_bytes=VMEM_LIMIT),
        name="in_proj",
    )(x2, wt, wr, w2h, w2l, gb, cost, sint, cosr, sinr)


KEY_LOWEST_FINITE = -2139095040
ATTN_TQ = 256
ATTN_TK = 256


def _key_to_float(key):
    return pltpu.bitcast(key ^ ((key >> 31) & 0x7FFFFFFF), F32)


def _attn_kernel(qt_ref, qit_ref, wit_ref, sag_ref, k_ref, ki_ref, vt_ref,
                 o_ref, sc_ref, hi_ref, qm_ref, acc_ref, cut_ref, s_ref, p_ref, *, tq, tk, topk, seq):
    i = pl.program_id(1)
    nchunks = (i + 1) * (tq // tk)
    rel = (lax.broadcasted_iota(I32, (tk, tq), 0) - lax.broadcasted_iota(I32, (tk, tq), 1))

    def score_pair(t, carry):
        js = (2 * t, 2 * t + 1)
        starts = [pl.multiple_of(j * tk, tk) for j in js]
        kics = [ki_ref[pl.ds(ks, tk), :] for ks in starts]
        scs = [jnp.zeros((tk, tq), F32) for _ in js]
        for h in range(IDX_HEADS):
            for n in range(2):
                lg = _dot(kics[n], qit_ref[h * IDX_DIM:(h + 1) * IDX_DIM, :])
                scs[n] = scs[n] + wit_ref[h:h + 1, :] * jnp.maximum(lg, 0.0)
        for n in range(2):
            sc = jnp.where(rel <= i * tq - js[n] * tk, scs[n], -jnp.inf)
            sc_ref[pl.ds(starts[n], tk), :] = sc
            upper = pltpu.bitcast(pltpu.bitcast(sc, I32) & -65536, F32)
            hi_ref[pl.ds(starts[n], tk), :] = upper.astype(BF16)
        return carry

    lax.fori_loop(0, (nchunks + 1) // 2, score_pair, 0)

    packed_rows = 16
    one16 = jnp.ones((tk, tq), BF16)
    zero16 = jnp.zeros((tk, tq), BF16)

    def count_upper(cand):
        def body(t, acc):
            for j in (2 * t, 2 * t + 1):
                ks = pl.multiple_of(j * tk, tk)
                hit = jnp.where(hi_ref[pl.ds(ks, tk), :] >= cand, one16, zero16)
                hit = hit.reshape(tk // (packed_rows * REDUCE_WAYS), REDUCE_WAYS, packed_rows, tq)
                part = hit[0]
                for r in range(1, hit.shape[0]):
                    part = part + hit[r]
                acc = acc + part.astype(F32)
            return acc
        acc = lax.fori_loop(0, (nchunks + 1) // 2, body,
                            jnp.zeros((REDUCE_WAYS, packed_rows, tq), F32))
        return acc.sum(axis=0).sum(axis=0, keepdims=True).astype(I32)

    def upper_key_to_bf16(key16):
        pattern = (key16 ^ ((key16 >> 31) & 0x7FFF)) & 0xFFFF
        return pltpu.bitcast(jnp.left_shift(pattern, 16), F32).astype(BF16)

    def count(hits):
        def body(t, acc):
            for j in (2 * t, 2 * t + 1):
                ks = pl.multiple_of(j * tk, tk)
                acc = acc + _col_partial(hits(sc_ref[pl.ds(ks, tk), :], ks), jnp.sum)
            return acc
        acc = lax.fori_loop(0, (nchunks + 1) // 2, body, jnp.zeros((REDUCE_WAYS, 8, tq), I32))
        return acc.sum(axis=0).sum(axis=0, keepdims=True)

    def count_ge(thr):
        return count(lambda s, ks: jnp.where(s >= thr, 1, 0))

    zero = jnp.zeros((1, tq), I32)
    cnt0 = count_upper(jnp.zeros((1, tq), BF16))
    tau = jnp.where(cnt0 >= topk, zero, -32768)
    cnt_tau = jnp.where(cnt0 >= topk, cnt0, zero)

    def upper_step(it, carry):
        tau, cnt_tau = carry
        cand = tau + jnp.left_shift(jnp.int32(1), 14 - it)
        cnt = count_upper(upper_key_to_bf16(cand))
        take = cnt >= topk
        return jnp.where(take, cand, tau), jnp.where(take, cnt, cnt_tau)

    tau, cnt_tau = lax.fori_loop(0, 15, upper_step, (tau, cnt_tau))
    tau = jnp.left_shift(tau, 16)

    def lower_step(it, carry):
        tau, cnt_tau = carry
        cand = tau + jnp.left_shift(jnp.int32(1), 15 - it)
        cnt = count_ge(_key_to_float(cand))
        take = cnt >= topk
        return jnp.where(take, cand, tau), jnp.where(take, cnt, cnt_tau)

    tau, cnt_tau = lax.fori_loop(0, 16, lower_step, (tau, cnt_tau))
    thr = _key_to_float(jnp.maximum(tau, KEY_LOWEST_FINITE))

    has_ties = jnp.max(cnt_tau) > topk

    @pl.when(jnp.logical_not(has_ties))
    def _():
        def bias_chunk(j, carry):
            ks = pl.multiple_of(j * tk, tk)
            sc_ref[pl.ds(ks, tk), :] = jnp.where(sc_ref[pl.ds(ks, tk), :] >= thr, 0.0, NEG_BIG)
            return carry

        lax.fori_loop(0, nchunks, bias_chunk, 0)

    @pl.when(has_ties)
    def _():
        need = topk - count(lambda s, ks: jnp.where(s > thr, 1, 0))
        row = lax.broadcasted_iota(I32, (tk, tq), 0)
        nbits = (seq - 1).bit_length()

        def idx_step(it, x):
            cand = x + jnp.left_shift(jnp.int32(1), nbits - 1 - it)
            below = count(lambda s, ks: jnp.where(s == thr, jnp.where(row + ks < cand, 1, 0), 0))
            return jnp.where(below < need, cand, x)

        cut_ref[...] = lax.fori_loop(0, nbits, idx_step, zero)

        def bias_chunk(j, carry):
            ks = pl.multiple_of(j * tk, tk)
            s = sc_ref[pl.ds(ks, tk), :]
            tied = jnp.where(row + ks <= cut_ref[...], 0.0, NEG_BIG)
            sc_ref[pl.ds(ks, tk), :] = jnp.where(s > thr, 0.0, jnp.where(s == thr, tied, NEG_BIG))
            return carry

        lax.fori_loop(0, nchunks, bias_chunk, 0)

    head_row = lax.broadcasted_iota(I32, (LANES, tq), 0) // A_HEAD_DIM
    for h in range(A_HEADS):
        qpair = qt_ref[(h // 2) * LANES:(h // 2 + 1) * LANES, :]
        qm_ref[h] = jnp.where(head_row == (h % 2), qpair, jnp.zeros_like(qpair))
    acc_ref[...] = jnp.zeros(acc_ref.shape, F32)

    last_chunk = seq // tk - 1

    @pl.when((nchunks % 2 == 1) & (nchunks <= last_chunk))
    def _():
        sc_ref[pl.ds(pl.multiple_of(nchunks * tk, tk), tk), :] = jnp.full((tk, tq), NEG_BIG, F32)

    def step(j, slot, m_run, l_run, top_cur, alpha_prev):
        other = 1 - slot
        ks_a = pl.multiple_of(jnp.minimum(j + 1, last_chunk) * tk, tk)
        ks_c = pl.multiple_of(jnp.maximum(j - 1, 0) * tk, tk)
        m_new = jnp.maximum(m_run, top_cur)
        alpha = jnp.exp2(m_run - m_new)
        tops, sums = [], []
        for h in range(A_HEADS):
            hs = slice(h * A_HEAD_DIM, (h + 1) * A_HEAD_DIM)
            kc = k_ref[pl.ds(ks_a, tk), (h // 2) * LANES:(h // 2 + 1) * LANES]
            s = _dot(kc, qm_ref[h]) + sc_ref[pl.ds(ks_a, tk), :]
            s_ref[other * A_HEADS + h] = s
            tops.append(_col_reduce(s, jnp.max))
            pv = _dot(vt_ref[hs, pl.ds(ks_c, tk)], p_ref[other * A_HEADS + h])
            acc_ref[hs, :] = alpha_prev[h:h + 1, :] * acc_ref[hs, :] + pv
            p = jnp.exp2(s_ref[slot * A_HEADS + h] - m_new[h:h + 1, :])
            p_ref[slot * A_HEADS + h] = p.astype(BF16)
            sums.append(_col_reduce(p, jnp.sum))
        l_new = alpha * l_run + jnp.concatenate(sums, axis=0)
        return m_new, l_new, jnp.concatenate(tops, axis=0), alpha

    def attn_pair(t, carry):
        carry = step(2 * t, 0, *carry)
        return step(2 * t + 1, 1, *carry)

    tops0 = []
    for h in range(A_HEADS):
        s = _dot(k_ref[0:tk, (h // 2) * LANES:(h // 2 + 1) * LANES], qm_ref[h]) + sc_ref[0:tk, :]
        s_ref[h] = s
        tops0.append(_col_reduce(s, jnp.max))
    p_ref[A_HEADS:2 * A_HEADS] = jnp.zeros((A_HEADS, tk, tq), BF16)
    init = (jnp.full((A_HEADS, tq), NEG_BIG, F32), jnp.zeros((A_HEADS, tq), F32),
            jnp.concatenate(tops0, axis=0), jnp.ones((A_HEADS, tq), F32))
    npairs = (nchunks + 1) // 2
    _, l_fin, _, alpha_last = lax.fori_loop(0, npairs, attn_pair, init)
    ks_last = pl.multiple_of(jnp.minimum(2 * npairs - 1, last_chunk) * tk, tk)
    for h in range(A_HEADS):
        hs = slice(h * A_HEAD_DIM, (h + 1) * A_HEAD_DIM)
        pv = _dot(vt_ref[hs, pl.ds(ks_last, tk)], p_ref[A_HEADS + h])
        acc_ref[hs, :] = alpha_last[h:h + 1, :] * acc_ref[hs, :] + pv

    for h in range(A_HEADS):
        hs = slice(h * A_HEAD_DIM, (h + 1) * A_HEAD_DIM)
        acc_ref[hs, :] = acc_ref[hs, :] / l_fin[h:h + 1, :]
    o_ref[...] = (acc_ref[...].T * sag_ref[...].astype(F32)).astype(BF16)


def _attention(qt, qit, wit, sag, k, ki, vt, batch, seq):
    tq, tk = ATTN_TQ, ATTN_TK
    assert seq % (2 * tk) == 0 and tq % tk == 0
    nq = seq // tq
    topk = min(TOPK_MAX, seq // 4)
    once = pl.Buffered(1)
    return pl.pallas_call(
        functools.partial(_attn_kernel, tq=tq, tk=tk, topk=topk, seq=seq),
        grid=(batch, nq),
        in_specs=[pl.BlockSpec((A_WIDTH, tq), lambda b, i: (0, b * nq + i)),
                  pl.BlockSpec((A_WIDTH, tq), lambda b, i: (0, b * nq + i)),
                  pl.BlockSpec((IDX_HEADS, tq), lambda b, i: (0, b * nq + i)),
                  pl.BlockSpec((tq, A_WIDTH), lambda b, i: (b * nq + i, 0)),
                  pl.BlockSpec((seq, A_WIDTH), lambda b, i: (b, 0), pipeline_mode=once),
                  pl.BlockSpec((seq, IDX_DIM), lambda b, i: (b, 0), pipeline_mode=once),
                  pl.BlockSpec((A_WIDTH, seq), lambda b, i: (0, b), pipeline_mode=once)],
        out_specs=pl.BlockSpec((tq, A_WIDTH), lambda b, i: (b * nq + i, 0)),
        out_shape=jax.ShapeDtypeStruct((batch * seq, A_WIDTH), BF16),
        scratch_shapes=[pltpu.VMEM((seq, tq), F32),
                        pltpu.VMEM((seq, tq), BF16),
                        pltpu.VMEM((A_HEADS, LANES, tq), BF16),
                        pltpu.VMEM((A_WIDTH, tq), F32),
                        pltpu.VMEM((1, tq), I32),
                        pltpu.VMEM((2 * A_HEADS, tk, tq), F32),
                        pltpu.VMEM((2 * A_HEADS, tk, tq), BF16)],
        compiler_params=pltpu.CompilerParams(
            dimension_semantics=("arbitrary", "arbitrary"), vmem_limit_bytes=VMEM_LIMIT),
        name="dsa_attention",
    )(qt, qit, wit, sag, k, ki, vt)


CONV_HALO = 32
CONV_SUB = 64


def _conv_kernel(cur_ref, halo_ref, sbg_ref, cw_ref, cb_ref, lg_ref, lb_ref, pw_ref, pb_ref,
                 o_ref, buf_ref, *, tt):
    i = pl.program_id(1)
    halo = halo_ref[...]
    buf_ref[0, 0:CONV_HALO, :] = jnp.where(i == 0, jnp.zeros_like(halo), halo)
    buf_ref[0, CONV_HALO:CONV_HALO + tt, :] = cur_ref[...]
    lead = CONV_HALO - (CONV_WIDTH - 1)
    span = tt + CONV_HALO - SUBLANES
    for s in range(1, SUBLANES):
        buf_ref[s, 0:span, :] = buf_ref[0, s:s + span, :]
    for r in range(tt // CONV_SUB):
        acc = jnp.zeros((CONV_SUB, B_WIDTH), F32)
        for j in range(CONV_WIDTH):
            shift, base = (lead + j) % SUBLANES, (lead + j) // SUBLANES * SUBLANES
            start = r * CONV_SUB + base
            acc = acc + buf_ref[shift, start:start + CONV_SUB, :] * cw_ref[j:j + 1, :]
        hc = acc + cb_ref[...]
        mu = jnp.mean(hc, axis=-1, keepdims=True)
        var = jnp.mean(jnp.square(hc - mu), axis=-1, keepdims=True)
        hn = (hc - mu) * lax.rsqrt(var + EPS) * lg_ref[...] + lb_ref[...]
        act = _silu(hn).astype(BF16)
        yb = _dot(act, pw_ref[...]) + pb_ref[...]
        rows = slice(r * CONV_SUB, (r + 1) * CONV_SUB)
        o_ref[rows, :] = (yb * sbg_ref[rows, :].astype(F32)).astype(BF16)


def _conformer(glu, sbg, cw, cb, lg, lb, pw, pb, batch, seq):
    tt = 512
    nt = seq // tt
    per = tt // CONV_HALO
    vec = lambda a: pl.BlockSpec(a.shape, lambda b, i: (0, 0))
    return pl.pallas_call(
        functools.partial(_conv_kernel, tt=tt),
        grid=(batch, nt),
        in_specs=[pl.BlockSpec((tt, B_WIDTH), lambda b, i: (b * nt + i, 0)),
                  pl.BlockSpec((CONV_HALO, B_WIDTH),
                               lambda b, i: (jnp.maximum((b * nt + i) * per - 1, 0), 0)),
                  pl.BlockSpec((tt, B_WIDTH), lambda b, i: (b * nt + i, 0)),
                  vec(cw), vec(cb), vec(lg), vec(lb), vec(pw), vec(pb)],
        out_specs=pl.BlockSpec((tt, B_WIDTH), lambda b, i: (b * nt + i, 0)),
        out_shape=jax.ShapeDtypeStruct((batch * seq, B_WIDTH), BF16),
        scratch_shapes=[pltpu.VMEM((SUBLANES, CONV_HALO + tt, B_WIDTH), F32)],
        compiler_params=pltpu.CompilerParams(dimension_semantics=("arbitrary", "arbitrary")),
        name="conformer_conv",
    )(glu, glu, sbg, cw, cb, lg, lb, pw, pb)


def _gla_kernel(q_ref, k_ref, v_ref, g_ref, scg_ref, gn_ref, o_ref, st_ref, *, tg, nbatch):
    c = GLA_CHUNK
    nsub = c // GLA_SUB

    @pl.when(pl.program_id(0) == 0)
    def _():
        st_ref[...] = jnp.zeros_like(st_ref)

    ri = lax.broadcasted_iota(I32, (c, c), 0)
    ci = lax.broadcasted_iota(I32, (c, c), 1)
    tri = jnp.where(ri >= ci, 1.0, 0.0).astype(BF16)
    r2 = lax.broadcasted_iota(I32, (C_WIDTH, C_KEYS), 0)
    c2 = lax.broadcasted_iota(I32, (C_WIDTH, C_KEYS), 1)
    key_blk = (r2 // C_VAL_DIM) == (c2 // C_KEY_DIM)
    key_blk_sub = jnp.concatenate([key_blk] * nsub, axis=1)
    r3 = lax.broadcasted_iota(I32, (C_WIDTH, C_WIDTH), 0)
    c3 = lax.broadcasted_iota(I32, (C_WIDTH, C_WIDTH), 1)
    val_blk = (r3 // C_VAL_DIM) == (c3 // C_VAL_DIM)
    ones_blk = jnp.where(val_blk, 1.0, 0.0).astype(BF16)
    r4 = lax.broadcasted_iota(I32, (c, C_WIDTH), 0)
    c4 = lax.broadcasted_iota(I32, (c, C_WIDTH), 1)
    causal = (c4 % c) <= r4
    sub_of_row = lax.broadcasted_iota(I32, (c, C_KEYS), 0) // GLA_SUB

    def one_batch(bi, rows):
        q = q_ref[bi, rows, :]
        k = k_ref[bi, rows, :]
        v = v_ref[bi, rows, :]
        g1, g2, g3 = _split3(g_ref[bi, rows, :])
        b = _dot(tri, g1) + _dot(tri, g2) + _dot(tri, g3)
        b_last = b[c - 1:c, :]
        qe = (q * jnp.exp(b)).astype(BF16)
        kd = (k * jnp.exp(b_last - b)).astype(BF16)
        st = st_ref[bi]
        yield

        heads = [b[s * GLA_SUB:s * GLA_SUB + 1, :] for s in range(nsub)]
        own = heads[0]
        for s in range(1, nsub):
            own = jnp.where(sub_of_row >= s, heads[s], own)
        qx = q * jnp.exp(b - own)
        q_cat = jnp.concatenate([jnp.where(sub_of_row == s, qx, 0.0) for s in range(nsub)],
                                axis=1).astype(BF16)
        k_cat = jnp.concatenate(
            [jnp.where(sub_of_row <= s, k * jnp.exp(jnp.minimum(heads[s] - b, GLA_EXP_CLAMP)), 0.0)
             for s in range(nsub)], axis=1)
        k_bd = jnp.where(key_blk_sub, jnp.concatenate([k_cat] * C_HEADS, axis=0), 0.0)
        yield
        a = jnp.where(causal, _dot_nt(q_cat, k_bd.astype(BF16)), 0.0)
        v_bd = jnp.where(val_blk, jnp.concatenate([v] * C_HEADS, axis=0), 0.0).astype(BF16)
        o = _dot(a.astype(BF16), v_bd) + _dot_nt(qe, st.astype(BF16))

        yield
        upd = _dot(v.T.astype(BF16), kd)
        st_ref[bi] = st * jnp.exp(b_last) + jnp.where(key_blk, upd, 0.0)

        yield
        oh, ol, _ = _split3(o * o)
        ms = (_dot(oh, ones_blk) + _dot(ol, ones_blk)) * (1.0 / C_VAL_DIM)
        on = o * lax.rsqrt(ms + EPS) * gn_ref[...]
        o_ref[bi, rows, :] = (on * scg_ref[bi, rows, :].astype(F32)).astype(BF16)

    def chunk(n, carry):
        rows = pl.ds(pl.multiple_of(n * c, c), c)
        pending = [one_batch(bi, rows) for bi in range(nbatch)]
        while pending:
            pending = [gen for gen in pending if next(gen, StopIteration) is not StopIteration]
        return carry

    lax.fori_loop(0, tg // c, chunk, 0)


def _gla(cq, ck, cv, la, scg, gn, batch, seq):
    tg = 512
    blk = lambda w: pl.BlockSpec((batch, tg, w), lambda i: (0, i, 0))
    by_batch = lambda a: a.reshape(batch, seq, a.shape[-1])
    out = pl.pallas_call(
        functools.partial(_gla_kernel, tg=tg, nbatch=batch),
        grid=(seq // tg,),
        in_specs=[blk(C_KEYS), blk(C_KEYS), blk(C_WIDTH), blk(C_KEYS), blk(C_WIDTH),
                  pl.BlockSpec(gn.shape, lambda i: (0, 0))],
        out_specs=blk(C_WIDTH),
        out_shape=jax.ShapeDtypeStruct((batch, seq, C_WIDTH), BF16),
        scratch_shapes=[pltpu.VMEM((batch, C_WIDTH, C_KEYS), F32)],
        compiler_params=pltpu.CompilerParams(dimension_semantics=("arbitrary",)),
        name="gla",
    )(by_batch(cq), by_batch(ck), by_batch(cv), by_batch(la), by_batch(scg), gn)
    return out.reshape(batch * seq, C_WIDTH)


def _outproj_kernel(ya_ref, yb_ref, yc_ref, x_ref, wo_ref, g_ref, b_ref, o_ref, *, alpha):
    y = (_dot(ya_ref[...], wo_ref[0:A_WIDTH, :])
         + _dot(yb_ref[...], wo_ref[A_WIDTH:A_WIDTH + B_WIDTH, :])
         + _dot(yc_ref[...], wo_ref[A_WIDTH + B_WIDTH:, :]))
    z = alpha * x_ref[...] + y
    mu = jnp.mean(z, axis=-1, keepdims=True)
    var = jnp.mean(jnp.square(z - mu), axis=-1, keepdims=True)
    o_ref[...] = (z - mu) * lax.rsqrt(var + EPS) * g_ref[...] + b_ref[...]


def _outproj(ya, yb, yc, x2, wo, g, b, alpha):
    m, d = x2.shape
    tm = 512
    row = lambda w: pl.BlockSpec((tm, w), lambda i: (i, 0))
    full = lambda a: pl.BlockSpec(a.shape, lambda i: (0, 0))
    return pl.pallas_call(
        functools.partial(_outproj_kernel, alpha=alpha),
        grid=(m // tm,),
        in_specs=[row(A_WIDTH), row(B_WIDTH), row(C_WIDTH), row(d), full(wo), full(g), full(b)],
        out_specs=row(d),
        out_shape=jax.ShapeDtypeStruct((m, d), F32),
        compiler_params=pltpu.CompilerParams(dimension_semantics=("arbitrary",)),
        name="out_proj",
    )(ya, yb, yc, x2, wo, g, b)


def _pack_in_weights(w):
    sizes = (A_WIDTH, A_WIDTH, A_WIDTH, A_WIDTH, IDX_HEADS * IDX_DIM, IDX_DIM, IDX_HEADS,
             2 * B_WIDTH, B_WIDTH, C_KEYS, C_KEYS, C_WIDTH, C_WIDTH, GATE_RANK)
    offs = np.concatenate([[0], np.cumsum(sizes)])
    (a_q, a_k, a_v, a_g, i_q, i_k, i_w, b_glu, b_g, c_q, c_k, c_v, c_g, c_lr) = [
        w[:, offs[n]:offs[n + 1]] for n in range(len(sizes))]
    d = w.shape[0]
    pad = lambda a, n: jnp.concatenate([a, jnp.zeros((d, n - a.shape[1]), a.dtype)], axis=1)
    wt = jnp.concatenate([a_q, i_q, a_v, pad(i_w, T_END - T_WI)], axis=1).T
    wr = jnp.concatenate([a_k, pad(i_k, LANES), a_g, b_glu, b_g, c_q, c_k, c_v, c_g,
                          pad(c_lr, LANES)], axis=1)
    return wt.astype(BF16), wr.astype(BF16)


def kernel(x, positions, w_in, conv_w, conv_b, cln_g, cln_b, pw_w, pw_b, gate_w2, gate_b,
           gnorm_g, w_out, ln_g, ln_b):
    batch, seq, d = x.shape
    depth = w_in.shape[0]
    alpha = float((2 * depth) ** 0.25)
    m = batch * seq
    tables = _rope_tables(positions)
    x2 = x.reshape(m, d)
    for l in range(depth):
        wt, wr = _pack_in_weights(w_in[l])
        w2 = jnp.zeros((LANES, C_KEYS), F32).at[:GATE_RANK].set(gate_w2[l])
        w2h = w2.astype(BF16)
        w2l = (w2 - w2h.astype(F32)).astype(BF16)
        (qt, qit, vt, wit, k, ki, sag, glu, sbg, cq, ck, cv, scg, la) = _inproj(
            x2, wt, wr, w2h, w2l, gate_b[l].reshape(1, C_KEYS), tables)
        ya = _attention(qt, qit, wit, sag, k, ki, vt, batch, seq)
        yb = _conformer(glu, sbg, conv_w[l], conv_b[l].reshape(1, -1), cln_g[l].reshape(1, -1),
                        cln_b[l].reshape(1, -1), pw_w[l].astype(BF16), pw_b[l].reshape(1, -1),
                        batch, seq)
        yc = _gla(cq, ck, cv, la, scg, gnorm_g[l].reshape(1, -1), batch, seq)
        x2 = _outproj(ya, yb, yc, x2, w_out[l].astype(BF16), ln_g[l].reshape(1, -1),
                      ln_b[l].reshape(1, -1), alpha)
    return x2.reshape(batch, seq, d)
```

```python
import functools

import numpy as np
import jax
import jax.numpy as jnp
from jax import lax
from jax.experimental import pallas as pl
from jax.experimental.pallas import tpu as pltpu

F32 = jnp.float32
BF16 = jnp.bfloat16
I32 = jnp.int32

A_HEADS = 8
A_HEAD_DIM = 64
A_WIDTH = A_HEADS * A_HEAD_DIM
IDX_HEADS = 8
IDX_DIM = 64
TOPK_MAX = 256
B_WIDTH = 256
CONV_WIDTH = 31
C_HEADS = 4
C_KEY_DIM = 32
C_VAL_DIM = 64
C_KEYS = C_HEADS * C_KEY_DIM
C_WIDTH = C_HEADS * C_VAL_DIM
GATE_RANK = 16
GATE_TAU = 16.0
GLA_CHUNK = 64
GLA_SUB = 16
GLA_EXP_CLAMP = 80.0
ROPE_THETA = 10000.0
ROPE_HALF = 32
EPS = 1e-5

LANES = 128
SUBLANES = 8
VMEM_LIMIT = 56 * 1024 * 1024
INT_MIN = int(np.iinfo(np.int32).min)
NEG_BIG = -1e30
Q_SCALE = A_HEAD_DIM ** -0.5 * float(np.log2(np.e))

R_K, R_KI, R_AG, R_GV, R_GG, R_BG, R_CQ, R_CK, R_CV, R_CG, R_LR, R_END = (
    0, 512, 640, 1152, 1408, 1664, 1920, 2048, 2176, 2432, 2688, 2816)
T_Q, T_QI, T_V, T_WI, T_END = 0, 512, 1024, 1536, 1552


def _dot(a, b):
    return jnp.dot(a, b, preferred_element_type=F32)


def _dot_nt(a, b):
    return lax.dot_general(a, b, (((1,), (1,)), ((), ())), preferred_element_type=F32)


def _sigmoid(x):
    return 1.0 / (1.0 + jnp.exp(-x))


def _silu(x):
    return x * _sigmoid(x)


REDUCE_WAYS = 4


def _col_partial(x, op):
    r, c = x.shape
    return op(x.reshape(r // (8 * REDUCE_WAYS), REDUCE_WAYS, 8, c), axis=0)


def _col_reduce(x, op):
    return op(op(_col_partial(x, op), axis=0), axis=0, keepdims=True)


def _split3(x):
    a = x.astype(BF16)
    r = x - a.astype(F32)
    b = r.astype(BF16)
    c = (r - b.astype(F32)).astype(BF16)
    return a, b, c


def _rope_kernel(post_ref, posc_ref, invc_ref, invr_ref, sgn_ref,
                 cost_ref, sint_ref, cosr_ref, sinr_ref):
    ang_t = invc_ref[...] * post_ref[...]
    cost_ref[...] = jnp.cos(ang_t)
    sint_ref[...] = jnp.sin(ang_t)
    ang_r = posc_ref[...] * invr_ref[...]
    cosr_ref[...] = jnp.cos(ang_r)
    sinr_ref[...] = jnp.sin(ang_r) * sgn_ref[...]


def _rope_tables(positions):
    m = positions.size
    tm = 1024
    pos = positions.reshape(-1).astype(F32)
    inv = ROPE_THETA ** (-jnp.arange(0, 2 * ROPE_HALF, 2, dtype=F32) / (2 * ROPE_HALF))
    lane = np.arange(LANES)
    sgn = jnp.asarray(np.where(lane % 64 < ROPE_HALF, -1.0, 1.0), F32).reshape(1, LANES)
    inv_r = jnp.tile(inv, LANES // ROPE_HALF).reshape(1, LANES)
    return pl.pallas_call(
        _rope_kernel,
        grid=(m // tm,),
        in_specs=[pl.BlockSpec((1, tm), lambda i: (0, i)),
                  pl.BlockSpec((tm, 1), lambda i: (i, 0)),
                  pl.BlockSpec((ROPE_HALF, 1), lambda i: (0, 0)),
                  pl.BlockSpec((1, LANES), lambda i: (0, 0)),
                  pl.BlockSpec((1, LANES), lambda i: (0, 0))],
        out_specs=[pl.BlockSpec((ROPE_HALF, tm), lambda i: (0, i)),
                   pl.BlockSpec((ROPE_HALF, tm), lambda i: (0, i)),
                   pl.BlockSpec((tm, LANES), lambda i: (i, 0)),
                   pl.BlockSpec((tm, LANES), lambda i: (i, 0))],
        out_shape=[jax.ShapeDtypeStruct((ROPE_HALF, m), F32),
                   jax.ShapeDtypeStruct((ROPE_HALF, m), F32),
                   jax.ShapeDtypeStruct((m, LANES), F32),
                   jax.ShapeDtypeStruct((m, LANES), F32)],
        name="rope_tables",
    )(pos.reshape(1, m), pos.reshape(m, 1), inv.reshape(ROPE_HALF, 1), inv_r, sgn)


def _inproj_kernel(x_ref, wt_ref, wr_ref, w2h_ref, w2l_ref, gb_ref,
                   cost_ref, sint_ref, cosr_ref, sinr_ref,
                   qt_ref, qit_ref, vt_ref, wit_ref, k_ref, ki_ref, sag_ref,
                   glu_ref, sbg_ref, cq_ref, ck_ref, cv_ref, scg_ref, la_ref):
    xb = x_ref[...].astype(BF16)
    ct = cost_ref[...]
    st = sint_ref[...]

    ht = _dot_nt(wt_ref[...], xb)
    for h in range(A_HEADS):
        for base, out, scale in ((T_Q, qt_ref, Q_SCALE), (T_QI, qit_ref, 1.0)):
            r0 = base + h * 64
            x1 = ht[r0:r0 + 32]
            x2 = ht[r0 + 32:r0 + 64]
            o0 = h * 64
            out[o0:o0 + 32, :] = ((x1 * ct - x2 * st) * scale).astype(BF16)
            out[o0 + 32:o0 + 64, :] = ((x2 * ct + x1 * st) * scale).astype(BF16)
    vt_ref[...] = ht[T_V:T_V + A_WIDTH].astype(BF16)
    wit_ref[...] = ht[T_WI:T_WI + IDX_HEADS] * (IDX_HEADS ** -0.5 * IDX_DIM ** -0.5)

    cr = cosr_ref[...]
    sr = sinr_ref[...]
    lane = lax.broadcasted_iota(I32, cr.shape, 1)
    first_half = (lane % 64) < ROPE_HALF

    def rope_rows(v):
        fwd = pltpu.roll(v, LANES - ROPE_HALF, 1)
        bwd = pltpu.roll(v, ROPE_HALF, 1)
        return v * cr + jnp.where(first_half, fwd, bwd) * sr

    for g in range(A_WIDTH // LANES):
        c0 = R_K + g * LANES
        hk = _dot(xb, wr_ref[:, c0:c0 + LANES])
        k_ref[:, g * LANES:(g + 1) * LANES] = rope_rows(hk).astype(BF16)
    hki = _dot(xb, wr_ref[:, R_KI:R_KI + LANES])
    ki_ref[...] = rope_rows(hki)[:, :IDX_DIM].astype(BF16)

    sag_ref[...] = _silu(_dot(xb, wr_ref[:, R_AG:R_AG + A_WIDTH])).astype(BF16)
    gv = _dot(xb, wr_ref[:, R_GV:R_GV + B_WIDTH])
    gg = _dot(xb, wr_ref[:, R_GG:R_GG + B_WIDTH])
    glu_ref[...] = gv * _sigmoid(gg)
    sbg_ref[...] = _silu(_dot(xb, wr_ref[:, R_BG:R_BG + B_WIDTH])).astype(BF16)
    cq_ref[...] = _dot(xb, wr_ref[:, R_CQ:R_CQ + C_KEYS]) * (C_KEY_DIM ** -0.5)
    ck_ref[...] = _dot(xb, wr_ref[:, R_CK:R_CK + C_KEYS])
    cv_ref[...] = _dot(xb, wr_ref[:, R_CV:R_CV + C_WIDTH])
    scg_ref[...] = _silu(_dot(xb, wr_ref[:, R_CG:R_CG + C_WIDTH])).astype(BF16)

    lr = _dot(xb, wr_ref[:, R_LR:R_LR + LANES])
    lr_h = lr.astype(BF16)
    lr_l = (lr - lr_h.astype(F32)).astype(BF16)
    z = (_dot(lr_h, w2h_ref[...]) + _dot(lr_l, w2h_ref[...]) + _dot(lr_h, w2l_ref[...])
         + gb_ref[...])
    log_sig = jnp.minimum(z, 0.0) - jnp.log(1.0 + jnp.exp(-jnp.abs(z)))
    la_ref[...] = log_sig * (1.0 / GATE_TAU)


def _inproj(x2, wt, wr, w2h, w2l, gb, tables):
    m, d = x2.shape
    tm = 512
    cost, sint, cosr, sinr = tables
    row = lambda w: pl.BlockSpec((tm, w), lambda i: (i, 0))
    col = lambda r: pl.BlockSpec((r, tm), lambda i: (0, i))
    full = lambda a: pl.BlockSpec(a.shape, lambda i: (0, 0))
    outs = [(col(A_WIDTH), (A_WIDTH, m), BF16),
            (col(A_WIDTH), (A_WIDTH, m), BF16),
            (col(A_WIDTH), (A_WIDTH, m), BF16),
            (col(IDX_HEADS), (IDX_HEADS, m), F32),
            (row(A_WIDTH), (m, A_WIDTH), BF16),
            (row(IDX_DIM), (m, IDX_DIM), BF16),
            (row(A_WIDTH), (m, A_WIDTH), BF16),
            (row(B_WIDTH), (m, B_WIDTH), F32),
            (row(B_WIDTH), (m, B_WIDTH), BF16),
            (row(C_KEYS), (m, C_KEYS), F32),
            (row(C_KEYS), (m, C_KEYS), F32),
            (row(C_WIDTH), (m, C_WIDTH), F32),
            (row(C_WIDTH), (m, C_WIDTH), BF16),
            (row(C_KEYS), (m, C_KEYS), F32)]
    return pl.pallas_call(
        _inproj_kernel,
        grid=(m // tm,),
        in_specs=[row(d), full(wt), full(wr), full(w2h), full(w2l), full(gb),
                  col(ROPE_HALF), col(ROPE_HALF), row(LANES), row(LANES)],
        out_specs=[o[0] for o in outs],
        out_shape=[jax.ShapeDtypeStruct(o[1], o[2]) for o in outs],
        compiler_params=pltpu.CompilerParams(
            dimension_semantics=("arbitrary",), vmem_limit_bytes=VMEM_LIMIT),
        name="in_proj",
    )(x2, wt, wr, w2h, w2l, gb, cost, sint, cosr, sinr)


KEY_LOWEST_FINITE = -2139095040
ATTN_TQ = 256
ATTN_TK = 256


def _key_to_float(key):
    return pltpu.bitcast(key ^ ((key >> 31) & 0x7FFFFFFF), F32)


def _attn_kernel(qt_ref, qit_ref, wit_ref, sag_ref, k_ref, ki_ref, vt_ref,
                 o_ref, sc_ref, hi_ref, qm_ref, acc_ref, tie_ref, s_ref, p_ref, *, tq, tk, topk, seq):
    i = pl.program_id(1)
    nchunks = (i + 1) * (tq // tk)
    rel = (lax.broadcasted_iota(I32, (tk, tq), 0) - lax.broadcasted_iota(I32, (tk, tq), 1))

    def score_pair(t, carry):
        js = (2 * t, 2 * t + 1)
        starts = [pl.multiple_of(j * tk, tk) for j in js]
        kics = [ki_ref[pl.ds(ks, tk), :] for ks in starts]
        scs = [jnp.zeros((tk, tq), F32) for _ in js]
        for h in range(IDX_HEADS):
            for n in range(2):
                lg = _dot(kics[n], qit_ref[h * IDX_DIM:(h + 1) * IDX_DIM, :])
                scs[n] = scs[n] + wit_ref[h:h + 1, :] * jnp.maximum(lg, 0.0)
        for n in range(2):
            sc = jnp.where(rel <= i * tq - js[n] * tk, scs[n], -jnp.inf)
            sc_ref[pl.ds(starts[n], tk), :] = sc
            upper = pltpu.bitcast(pltpu.bitcast(sc, I32) & -65536, F32)
            hi_ref[pl.ds(starts[n], tk), :] = upper.astype(BF16)
        return carry

    lax.fori_loop(0, (nchunks + 1) // 2, score_pair, 0)

    packed_rows = 16
    one16 = jnp.ones((tk, tq), BF16)
    zero16 = jnp.zeros((tk, tq), BF16)

    def count_upper(cand):
        def body(t, acc):
            for j in (2 * t, 2 * t + 1):
                ks = pl.multiple_of(j * tk, tk)
                hit = jnp.where(hi_ref[pl.ds(ks, tk), :] >= cand, one16, zero16)
                hit = hit.reshape(tk // (packed_rows * REDUCE_WAYS), REDUCE_WAYS, packed_rows, tq)
                part = hit[0]
                for r in range(1, hit.shape[0]):
                    part = part + hit[r]
                acc = acc + part.astype(F32)
            return acc
        acc = lax.fori_loop(0, (nchunks + 1) // 2, body,
                            jnp.zeros((REDUCE_WAYS, packed_rows, tq), F32))
        return acc.sum(axis=0).sum(axis=0, keepdims=True).astype(I32)

    def upper_key_to_bf16(key16):
        pattern = (key16 ^ ((key16 >> 31) & 0x7FFF)) & 0xFFFF
        return pltpu.bitcast(jnp.left_shift(pattern, 16), F32).astype(BF16)

    def count_ge(cand):
        def body(t, acc):
            for j in (2 * t, 2 * t + 1):
                ks = pl.multiple_of(j * tk, tk)
                hit = jnp.where(sc_ref[pl.ds(ks, tk), :] >= cand, 1, 0)
                acc = acc + _col_partial(hit, jnp.sum)
            return acc
        acc = lax.fori_loop(0, (nchunks + 1) // 2, body, jnp.zeros((REDUCE_WAYS, 8, tq), I32))
        return acc.sum(axis=0).sum(axis=0, keepdims=True)

    zero = jnp.zeros((1, tq), I32)
    cnt0 = count_upper(jnp.zeros((1, tq), BF16))
    tau = jnp.where(cnt0 >= topk, zero, -32768)
    cnt_tau = jnp.where(cnt0 >= topk, cnt0, zero)

    def upper_step(it, carry):
        tau, cnt_tau = carry
        cand = tau + jnp.left_shift(jnp.int32(1), 14 - it)
        cnt = count_upper(upper_key_to_bf16(cand))
        take = cnt >= topk
        return jnp.where(take, cand, tau), jnp.where(take, cnt, cnt_tau)

    tau, cnt_tau = lax.fori_loop(0, 15, upper_step, (tau, cnt_tau))
    tau = jnp.left_shift(tau, 16)

    def lower_step(it, carry):
        tau, cnt_tau = carry
        cand = tau + jnp.left_shift(jnp.int32(1), 15 - it)
        cnt = count_ge(_key_to_float(cand))
        take = cnt >= topk
        return jnp.where(take, cand, tau), jnp.where(take, cnt, cnt_tau)

    tau, cnt_tau = lax.fori_loop(0, 16, lower_step, (tau, cnt_tau))
    thr = _key_to_float(jnp.maximum(tau, KEY_LOWEST_FINITE))

    has_ties = jnp.max(cnt_tau) > topk

    @pl.when(jnp.logical_not(has_ties))
    def _():
        def bias_chunk(j, carry):
            ks = pl.multiple_of(j * tk, tk)
            sc_ref[pl.ds(ks, tk), :] = jnp.where(sc_ref[pl.ds(ks, tk), :] >= thr, 0.0, NEG_BIG)
            return carry

        lax.fori_loop(0, nchunks, bias_chunk, 0)

    @pl.when(has_ties)
    def _():
        npairs = (nchunks + 1) // 2

        def mark(t, total):
            for j in (2 * t, 2 * t + 1):
                ks = pl.multiple_of(j * tk, tk)
                tie = jnp.where(sc_ref[pl.ds(ks, tk), :] == thr, 1.0, 0.0)
                hi_ref[pl.ds(ks, tk), :] = tie.astype(BF16)
                part = _col_partial(tie, jnp.sum).sum(axis=0)
                tie_ref[j] = part
                total = total + part
            return total

        n_tie = lax.fori_loop(0, npairs, mark, jnp.zeros((8, tq), F32)).sum(axis=0, keepdims=True)
        need = (topk - cnt_tau).astype(F32) + n_tie
        r_i = lax.broadcasted_iota(I32, (tk, tk), 0)
        c_i = lax.broadcasted_iota(I32, (tk, tk), 1)
        upto = jnp.where(c_i <= r_i, 1.0, 0.0).astype(BF16)

        def bias_pair(t, before):
            starts = [pl.multiple_of(j * tk, tk) for j in (2 * t, 2 * t + 1)]
            inside = [_dot(upto, hi_ref[pl.ds(ks, tk), :]) for ks in starts]
            for n, ks in enumerate(starts):
                s = sc_ref[pl.ds(ks, tk), :]
                tied = jnp.where(before + inside[n] <= need, 0.0, NEG_BIG)
                sc_ref[pl.ds(ks, tk), :] = jnp.where(s > thr, 0.0,
                                                     jnp.where(s == thr, tied, NEG_BIG))
                before = before + tie_ref[2 * t + n].sum(axis=0, keepdims=True)
            return before

        lax.fori_loop(0, npairs, bias_pair, jnp.zeros((1, tq), F32))

    head_row = lax.broadcasted_iota(I32, (LANES, tq), 0) // A_HEAD_DIM
    for h in range(A_HEADS):
        qpair = qt_ref[(h // 2) * LANES:(h // 2 + 1) * LANES, :]
        qm_ref[h] = jnp.where(head_row == (h % 2), qpair, jnp.zeros_like(qpair))
    acc_ref[...] = jnp.zeros(acc_ref.shape, F32)

    last_chunk = seq // tk - 1

    @pl.when((nchunks % 2 == 1) & (nchunks <= last_chunk))
    def _():
        sc_ref[pl.ds(pl.multiple_of(nchunks * tk, tk), tk), :] = jnp.full((tk, tq), NEG_BIG, F32)

    def step(j, slot, m_run, l_run, top_cur, alpha_prev):
        other = 1 - slot
        ks_a = pl.multiple_of(jnp.minimum(j + 1, last_chunk) * tk, tk)
        ks_c = pl.multiple_of(jnp.maximum(j - 1, 0) * tk, tk)
        m_new = jnp.maximum(m_run, top_cur)
        alpha = jnp.exp2(m_run - m_new)
        tops, sums = [], []
        for h in range(A_HEADS):
            hs = slice(h * A_HEAD_DIM, (h + 1) * A_HEAD_DIM)
            kc = k_ref[pl.ds(ks_a, tk), (h // 2) * LANES:(h // 2 + 1) * LANES]
            s = _dot(kc, qm_ref[h]) + sc_ref[pl.ds(ks_a, tk), :]
            s_ref[other * A_HEADS + h] = s
            tops.append(_col_reduce(s, jnp.max))
            pv = _dot(vt_ref[hs, pl.ds(ks_c, tk)], p_ref[other * A_HEADS + h])
            acc_ref[hs, :] = alpha_prev[h:h + 1, :] * acc_ref[hs, :] + pv
            p = jnp.exp2(s_ref[slot * A_HEADS + h] - m_new[h:h + 1, :])
            p_ref[slot * A_HEADS + h] = p.astype(BF16)
            sums.append(_col_reduce(p, jnp.sum))
        l_new = alpha * l_run + jnp.concatenate(sums, axis=0)
        return m_new, l_new, jnp.concatenate(tops, axis=0), alpha

    def attn_pair(t, carry):
        carry = step(2 * t, 0, *carry)
        return step(2 * t + 1, 1, *carry)

    tops0 = []
    for h in range(A_HEADS):
        s = _dot(k_ref[0:tk, (h // 2) * LANES:(h // 2 + 1) * LANES], qm_ref[h]) + sc_ref[0:tk, :]
        s_ref[h] = s
        tops0.append(_col_reduce(s, jnp.max))
    p_ref[A_HEADS:2 * A_HEADS] = jnp.zeros((A_HEADS, tk, tq), BF16)
    init = (jnp.full((A_HEADS, tq), NEG_BIG, F32), jnp.zeros((A_HEADS, tq), F32),
            jnp.concatenate(tops0, axis=0), jnp.ones((A_HEADS, tq), F32))
    npairs = (nchunks + 1) // 2
    _, l_fin, _, alpha_last = lax.fori_loop(0, npairs, attn_pair, init)
    ks_last = pl.multiple_of(jnp.minimum(2 * npairs - 1, last_chunk) * tk, tk)
    for h in range(A_HEADS):
        hs = slice(h * A_HEAD_DIM, (h + 1) * A_HEAD_DIM)
        pv = _dot(vt_ref[hs, pl.ds(ks_last, tk)], p_ref[A_HEADS + h])
        acc_ref[hs, :] = alpha_last[h:h + 1, :] * acc_ref[hs, :] + pv

    for h in range(A_HEADS):
        hs = slice(h * A_HEAD_DIM, (h + 1) * A_HEAD_DIM)
        acc_ref[hs, :] = acc_ref[hs, :] / l_fin[h:h + 1, :]
    o_ref[...] = (acc_ref[...].T * sag_ref[...].astype(F32)).astype(BF16)


def _attention(qt, qit, wit, sag, k, ki, vt, batch, seq):
    tq, tk = ATTN_TQ, ATTN_TK
    assert seq % (2 * tk) == 0 and tq % tk == 0
    nq = seq // tq
    topk = min(TOPK_MAX, seq // 4)
    once = pl.Buffered(1)
    return pl.pallas_call(
        functools.partial(_attn_kernel, tq=tq, tk=tk, topk=topk, seq=seq),
        grid=(batch, nq),
        in_specs=[pl.BlockSpec((A_WIDTH, tq), lambda b, i: (0, b * nq + i)),
                  pl.BlockSpec((A_WIDTH, tq), lambda b, i: (0, b * nq + i)),
                  pl.BlockSpec((IDX_HEADS, tq), lambda b, i: (0, b * nq + i)),
                  pl.BlockSpec((tq, A_WIDTH), lambda b, i: (b * nq + i, 0)),
                  pl.BlockSpec((seq, A_WIDTH), lambda b, i: (b, 0), pipeline_mode=once),
                  pl.BlockSpec((seq, IDX_DIM), lambda b, i: (b, 0), pipeline_mode=once),
                  pl.BlockSpec((A_WIDTH, seq), lambda b, i: (0, b), pipeline_mode=once)],
        out_specs=pl.BlockSpec((tq, A_WIDTH), lambda b, i: (b * nq + i, 0)),
        out_shape=jax.ShapeDtypeStruct((batch * seq, A_WIDTH), BF16),
        scratch_shapes=[pltpu.VMEM((seq, tq), F32),
                        pltpu.VMEM((seq, tq), BF16),
                        pltpu.VMEM((A_HEADS, LANES, tq), BF16),
                        pltpu.VMEM((A_WIDTH, tq), F32),
                        pltpu.VMEM((seq // tk, SUBLANES, tq), F32),
                        pltpu.VMEM((2 * A_HEADS, tk, tq), F32),
                        pltpu.VMEM((2 * A_HEADS, tk, tq), BF16)],
        compiler_params=pltpu.CompilerParams(
            dimension_semantics=("arbitrary", "arbitrary"), vmem_limit_bytes=VMEM_LIMIT),
        name="dsa_attention",
    )(qt, qit, wit, sag, k, ki, vt)


CONV_HALO = 32
CONV_SUB = 64


def _conv_kernel(cur_ref, halo_ref, sbg_ref, cw_ref, cb_ref, lg_ref, lb_ref, pw_ref, pb_ref,
                 o_ref, buf_ref, *, tt):
    i = pl.program_id(1)
    halo = halo_ref[...]
    buf_ref[0, 0:CONV_HALO, :] = jnp.where(i == 0, jnp.zeros_like(halo), halo)
    buf_ref[0, CONV_HALO:CONV_HALO + tt, :] = cur_ref[...]
    lead = CONV_HALO - (CONV_WIDTH - 1)
    span = tt + CONV_HALO - SUBLANES
    for s in range(1, SUBLANES):
        buf_ref[s, 0:span, :] = buf_ref[0, s:s + span, :]
    for r in range(tt // CONV_SUB):
        acc = jnp.zeros((CONV_SUB, B_WIDTH), F32)
        for j in range(CONV_WIDTH):
            shift, base = (lead + j) % SUBLANES, (lead + j) // SUBLANES * SUBLANES
            start = r * CONV_SUB + base
            acc = acc + buf_ref[shift, start:start + CONV_SUB, :] * cw_ref[j:j + 1, :]
        hc = acc + cb_ref[...]
        mu = jnp.mean(hc, axis=-1, keepdims=True)
        var = jnp.mean(jnp.square(hc - mu), axis=-1, keepdims=True)
        hn = (hc - mu) * lax.rsqrt(var + EPS) * lg_ref[...] + lb_ref[...]
        act = _silu(hn).astype(BF16)
        yb = _dot(act, pw_ref[...]) + pb_ref[...]
        rows = slice(r * CONV_SUB, (r + 1) * CONV_SUB)
        o_ref[rows, :] = (yb * sbg_ref[rows, :].astype(F32)).astype(BF16)


def _conformer(glu, sbg, cw, cb, lg, lb, pw, pb, batch, seq):
    tt = 512
    nt = seq // tt
    per = tt // CONV_HALO
    vec = lambda a: pl.BlockSpec(a.shape, lambda b, i: (0, 0))
    return pl.pallas_call(
        functools.partial(_conv_kernel, tt=tt),
        grid=(batch, nt),
        in_specs=[pl.BlockSpec((tt, B_WIDTH), lambda b, i: (b * nt + i, 0)),
                  pl.BlockSpec((CONV_HALO, B_WIDTH),
                               lambda b, i: (jnp.maximum((b * nt + i) * per - 1, 0), 0)),
                  pl.BlockSpec((tt, B_WIDTH), lambda b, i: (b * nt + i, 0)),
                  vec(cw), vec(cb), vec(lg), vec(lb), vec(pw), vec(pb)],
        out_specs=pl.BlockSpec((tt, B_WIDTH), lambda b, i: (b * nt + i, 0)),
        out_shape=jax.ShapeDtypeStruct((batch * seq, B_WIDTH), BF16),
        scratch_shapes=[pltpu.VMEM((SUBLANES, CONV_HALO + tt, B_WIDTH), F32)],
        compiler_params=pltpu.CompilerParams(dimension_semantics=("arbitrary", "arbitrary")),
        name="conformer_conv",
    )(glu, glu, sbg, cw, cb, lg, lb, pw, pb)


def _gla_kernel(q_ref, k_ref, v_ref, g_ref, scg_ref, gn_ref, o_ref, st_ref, *, tg, nbatch):
    c = GLA_CHUNK
    nsub = c // GLA_SUB

    @pl.when(pl.program_id(0) == 0)
    def _():
        st_ref[...] = jnp.zeros_like(st_ref)

    ri = lax.broadcasted_iota(I32, (c, c), 0)
    ci = lax.broadcasted_iota(I32, (c, c), 1)
    tri = jnp.where(ri >= ci, 1.0, 0.0).astype(BF16)
    r2 = lax.broadcasted_iota(I32, (C_WIDTH, C_KEYS), 0)
    c2 = lax.broadcasted_iota(I32, (C_WIDTH, C_KEYS), 1)
    key_blk = (r2 // C_VAL_DIM) == (c2 // C_KEY_DIM)
    key_blk_sub = jnp.concatenate([key_blk] * nsub, axis=1)
    r3 = lax.broadcasted_iota(I32, (C_WIDTH, C_WIDTH), 0)
    c3 = lax.broadcasted_iota(I32, (C_WIDTH, C_WIDTH), 1)
    val_blk = (r3 // C_VAL_DIM) == (c3 // C_VAL_DIM)
    ones_blk = jnp.where(val_blk, 1.0, 0.0).astype(BF16)
    r4 = lax.broadcasted_iota(I32, (c, C_WIDTH), 0)
    c4 = lax.broadcasted_iota(I32, (c, C_WIDTH), 1)
    causal = (c4 % c) <= r4
    sub_of_row = lax.broadcasted_iota(I32, (c, C_KEYS), 0) // GLA_SUB

    def one_batch(bi, rows):
        q = q_ref[bi, rows, :]
        k = k_ref[bi, rows, :]
        v = v_ref[bi, rows, :]
        g1, g2, g3 = _split3(g_ref[bi, rows, :])
        b = _dot(tri, g1) + _dot(tri, g2) + _dot(tri, g3)
        b_last = b[c - 1:c, :]
        qe = (q * jnp.exp(b)).astype(BF16)
        kd = (k * jnp.exp(b_last - b)).astype(BF16)
        st = st_ref[bi]
        yield

        heads = [b[s * GLA_SUB:s * GLA_SUB + 1, :] for s in range(nsub)]
        own = heads[0]
        for s in range(1, nsub):
            own = jnp.where(sub_of_row >= s, heads[s], own)
        qx = q * jnp.exp(b - own)
        q_cat = jnp.concatenate([jnp.where(sub_of_row == s, qx, 0.0) for s in range(nsub)],
                                axis=1).astype(BF16)
        k_cat = jnp.concatenate(
            [jnp.where(sub_of_row <= s, k * jnp.exp(jnp.minimum(heads[s] - b, GLA_EXP_CLAMP)), 0.0)
             for s in range(nsub)], axis=1)
        k_bd = jnp.where(key_blk_sub, jnp.concatenate([k_cat] * C_HEADS, axis=0), 0.0)
        yield
        a = jnp.where(causal, _dot_nt(q_cat, k_bd.astype(BF16)), 0.0)
        v_bd = jnp.where(val_blk, jnp.concatenate([v] * C_HEADS, axis=0), 0.0).astype(BF16)
        o = _dot(a.astype(BF16), v_bd) + _dot_nt(qe, st.astype(BF16))

        yield
        upd = _dot(v.T.astype(BF16), kd)
        st_ref[bi] = st * jnp.exp(b_last) + jnp.where(key_blk, upd, 0.0)

        yield
        oh, ol, _ = _split3(o * o)
        ms = (_dot(oh, ones_blk) + _dot(ol, ones_blk)) * (1.0 / C_VAL_DIM)
        on = o * lax.rsqrt(ms + EPS) * gn_ref[...]
        o_ref[bi, rows, :] = (on * scg_ref[bi, rows, :].astype(F32)).astype(BF16)

    def chunk(n, carry):
        rows = pl.ds(pl.multiple_of(n * c, c), c)
        pending = [one_batch(bi, rows) for bi in range(nbatch)]
        while pending:
            pending = [gen for gen in pending if next(gen, StopIteration) is not StopIteration]
        return carry

    lax.fori_loop(0, tg // c, chunk, 0)


def _gla(cq, ck, cv, la, scg, gn, batch, seq):
    tg = 512
    blk = lambda w: pl.BlockSpec((batch, tg, w), lambda i: (0, i, 0))
    by_batch = lambda a: a.reshape(batch, seq, a.shape[-1])
    out = pl.pallas_call(
        functools.partial(_gla_kernel, tg=tg, nbatch=batch),
        grid=(seq // tg,),
        in_specs=[blk(C_KEYS), blk(C_KEYS), blk(C_WIDTH), blk(C_KEYS), blk(C_WIDTH),
                  pl.BlockSpec(gn.shape, lambda i: (0, 0))],
        out_specs=blk(C_WIDTH),
        out_shape=jax.ShapeDtypeStruct((batch, seq, C_WIDTH), BF16),
        scratch_shapes=[pltpu.VMEM((batch, C_WIDTH, C_KEYS), F32)],
        compiler_params=pltpu.CompilerParams(dimension_semantics=("arbitrary",)),
        name="gla",
    )(by_batch(cq), by_batch(ck), by_batch(cv), by_batch(la), by_batch(scg), gn)
    return out.reshape(batch * seq, C_WIDTH)


def _outproj_kernel(ya_ref, yb_ref, yc_ref, x_ref, wo_ref, g_ref, b_ref, o_ref, *, alpha):
    y = (_dot(ya_ref[...], wo_ref[0:A_WIDTH, :])
         + _dot(yb_ref[...], wo_ref[A_WIDTH:A_WIDTH + B_WIDTH, :])
         + _dot(yc_ref[...], wo_ref[A_WIDTH + B_WIDTH:, :]))
    z = alpha * x_ref[...] + y
    mu = jnp.mean(z, axis=-1, keepdims=True)
    var = jnp.mean(jnp.square(z - mu), axis=-1, keepdims=True)
    o_ref[...] = (z - mu) * lax.rsqrt(var + EPS) * g_ref[...] + b_ref[...]


def _outproj(ya, yb, yc, x2, wo, g, b, alpha):
    m, d = x2.shape
    tm = 512
    row = lambda w: pl.BlockSpec((tm, w), lambda i: (i, 0))
    full = lambda a: pl.BlockSpec(a.shape, lambda i: (0, 0))
    return pl.pallas_call(
        functools.partial(_outproj_kernel, alpha=alpha),
        grid=(m // tm,),
        in_specs=[row(A_WIDTH), row(B_WIDTH), row(C_WIDTH), row(d), full(wo), full(g), full(b)],
        out_specs=row(d),
        out_shape=jax.ShapeDtypeStruct((m, d), F32),
        compiler_params=pltpu.CompilerParams(dimension_semantics=("arbitrary",)),
        name="out_proj",
    )(ya, yb, yc, x2, wo, g, b)


def _pack_in_weights(w):
    sizes = (A_WIDTH, A_WIDTH, A_WIDTH, A_WIDTH, IDX_HEADS * IDX_DIM, IDX_DIM, IDX_HEADS,
             2 * B_WIDTH, B_WIDTH, C_KEYS, C_KEYS, C_WIDTH, C_WIDTH, GATE_RANK)
    offs = np.concatenate([[0], np.cumsum(sizes)])
    (a_q, a_k, a_v, a_g, i_q, i_k, i_w, b_glu, b_g, c_q, c_k, c_v, c_g, c_lr) = [
        w[:, offs[n]:offs[n + 1]] for n in range(len(sizes))]
    d = w.shape[0]
    pad = lambda a, n: jnp.concatenate([a, jnp.zeros((d, n - a.shape[1]), a.dtype)], axis=1)
    wt = jnp.concatenate([a_q, i_q, a_v, pad(i_w, T_END - T_WI)], axis=1).T
    wr = jnp.concatenate([a_k, pad(i_k, LANES), a_g, b_glu, b_g, c_q, c_k, c_v, c_g,
                          pad(c_lr, LANES)], axis=1)
    return wt.astype(BF16), wr.astype(BF16)


def kernel(x, positions, w_in, conv_w, conv_b, cln_g, cln_b, pw_w, pw_b, gate_w2, gate_b,
           gnorm_g, w_out, ln_g, ln_b):
    batch, seq, d = x.shape
    depth = w_in.shape[0]
    alpha = float((2 * depth) ** 0.25)
    m = batch * seq
    tables = _rope_tables(positions)
    x2 = x.reshape(m, d)
    for l in range(depth):
        wt, wr = _pack_in_weights(w_in[l])
        w2 = jnp.zeros((LANES, C_KEYS), F32).at[:GATE_RANK].set(gate_w2[l])
        w2h = w2.astype(BF16)
        w2l = (w2 - w2h.astype(F32)).astype(BF16)
        (qt, qit, vt, wit, k, ki, sag, glu, sbg, cq, ck, cv, scg, la) = _inproj(
            x2, wt, wr, w2h, w2l, gate_b[l].reshape(1, C_KEYS), tables)
        ya = _attention(qt, qit, wit, sag, k, ki, vt, batch, seq)
        yb = _conformer(glu, sbg, conv_w[l], conv_b[l].reshape(1, -1), cln_g[l].reshape(1, -1),
                        cln_b[l].reshape(1, -1), pw_w[l].astype(BF16), pw_b[l].reshape(1, -1),
                        batch, seq)
        yc = _gla(cq, ck, cv, la, scg, gnorm_g[l].reshape(1, -1), batch, seq)
        x2 = _outproj(ya, yb, yc, x2, w_out[l].astype(BF16), ln_g[l].reshape(1, -1),
                      ln_b[l].reshape(1, -1), alpha)
    return x2.reshape(batch, seq, d)
```

```python
import functools

import numpy as np
import jax
import jax.numpy as jnp
from jax import lax
from jax.experimental import pallas as pl
from jax.experimental.pallas import tpu as pltpu

F32 = jnp.float32
BF16 = jnp.bfloat16
I32 = jnp.int32

A_HEADS = 8
A_HEAD_DIM = 64
A_WIDTH = A_HEADS * A_HEAD_DIM
IDX_HEADS = 8
IDX_DIM = 64
TOPK_MAX = 256
B_WIDTH = 256
CONV_WIDTH = 31
C_HEADS = 4
C_KEY_DIM = 32
C_VAL_DIM = 64
C_KEYS = C_HEADS * C_KEY_DIM
C_WIDTH = C_HEADS * C_VAL_DIM
GATE_RANK = 16
GATE_TAU = 16.0
GLA_CHUNK = 64
GLA_SUB = 16
GLA_EXP_CLAMP = 80.0
ROPE_THETA = 10000.0
ROPE_HALF = 32
EPS = 1e-5

LANES = 128
SUBLANES = 8
VMEM_LIMIT = 56 * 1024 * 1024
INT_MIN = int(np.iinfo(np.int32).min)
NEG_BIG = -1e30
Q_SCALE = A_HEAD_DIM ** -0.5 * float(np.log2(np.e))

R_K, R_KI, R_AG, R_GV, R_GG, R_BG, R_CQ, R_CK, R_CV, R_CG, R_LR, R_END = (
    0, 512, 640, 1152, 1408, 1664, 1920, 2048, 2176, 2432, 2688, 2816)
T_Q, T_QI, T_V, T_WI, T_END = 0, 512, 1024, 1536, 1552


def _dot(a, b):
    return jnp.dot(a, b, preferred_element_type=F32)


def _dot_nt(a, b):
    return lax.dot_general(a, b, (((1,), (1,)), ((), ())), preferred_element_type=F32)


def _sigmoid(x):
    return 1.0 / (1.0 + jnp.exp(-x))


def _silu(x):
    return x * _sigmoid(x)


REDUCE_WAYS = 4


def _col_partial(x, op):
    r, c = x.shape
    return op(x.reshape(r // (8 * REDUCE_WAYS), REDUCE_WAYS, 8, c), axis=0)


def _col_reduce(x, op):
    return op(op(_col_partial(x, op), axis=0), axis=0, keepdims=True)


def _split3(x):
    a = x.astype(BF16)
    r = x - a.astype(F32)
    b = r.astype(BF16)
    c = (r - b.astype(F32)).astype(BF16)
    return a, b, c


def _rope_kernel(post_ref, posc_ref, invc_ref, invr_ref, sgn_ref,
                 cost_ref, sint_ref, cosr_ref, sinr_ref):
    ang_t = invc_ref[...] * post_ref[...]
    cost_ref[...] = jnp.cos(ang_t)
    sint_ref[...] = jnp.sin(ang_t)
    ang_r = posc_ref[...] * invr_ref[...]
    cosr_ref[...] = jnp.cos(ang_r)
    sinr_ref[...] = jnp.sin(ang_r) * sgn_ref[...]


def _rope_tables(positions):
    m = positions.size
    tm = 1024
    pos = positions.reshape(-1).astype(F32)
    inv = ROPE_THETA ** (-jnp.arange(0, 2 * ROPE_HALF, 2, dtype=F32) / (2 * ROPE_HALF))
    lane = np.arange(LANES)
    sgn = jnp.asarray(np.where(lane % 64 < ROPE_HALF, -1.0, 1.0), F32).reshape(1, LANES)
    inv_r = jnp.tile(inv, LANES // ROPE_HALF).reshape(1, LANES)
    return pl.pallas_call(
        _rope_kernel,
        grid=(m // tm,),
        in_specs=[pl.BlockSpec((1, tm), lambda i: (0, i)),
                  pl.BlockSpec((tm, 1), lambda i: (i, 0)),
                  pl.BlockSpec((ROPE_HALF, 1), lambda i: (0, 0)),
                  pl.BlockSpec((1, LANES), lambda i: (0, 0)),
                  pl.BlockSpec((1, LANES), lambda i: (0, 0))],
        out_specs=[pl.BlockSpec((ROPE_HALF, tm), lambda i: (0, i)),
                   pl.BlockSpec((ROPE_HALF, tm), lambda i: (0, i)),
                   pl.BlockSpec((tm, LANES), lambda i: (i, 0)),
                   pl.BlockSpec((tm, LANES), lambda i: (i, 0))],
        out_shape=[jax.ShapeDtypeStruct((ROPE_HALF, m), F32),
                   jax.ShapeDtypeStruct((ROPE_HALF, m), F32),
                   jax.ShapeDtypeStruct((m, LANES), F32),
                   jax.ShapeDtypeStruct((m, LANES), F32)],
        name="rope_tables",
    )(pos.reshape(1, m), pos.reshape(m, 1), inv.reshape(ROPE_HALF, 1), inv_r, sgn)


def _inproj_kernel(x_ref, wt_ref, wr_ref, w2h_ref, w2l_ref, gb_ref,
                   cost_ref, sint_ref, cosr_ref, sinr_ref,
                   qt_ref, qit_ref, vt_ref, wit_ref, k_ref, ki_ref, sag_ref,
                   glu_ref, sbg_ref, cq_ref, ck_ref, cv_ref, scg_ref, la_ref):
    xb = x_ref[...].astype(BF16)
    ct = cost_ref[...]
    st = sint_ref[...]

    ht = _dot_nt(wt_ref[...], xb)
    for h in range(A_HEADS):
        for base, out, scale in ((T_Q, qt_ref, Q_SCALE), (T_QI, qit_ref, 1.0)):
            r0 = base + h * 64
            x1 = ht[r0:r0 + 32]
            x2 = ht[r0 + 32:r0 + 64]
            o0 = h * 64
            out[o0:o0 + 32, :] = ((x1 * ct - x2 * st) * scale).astype(BF16)
            out[o0 + 32:o0 + 64, :] = ((x2 * ct + x1 * st) * scale).astype(BF16)
    vt_ref[...] = ht[T_V:T_V + A_WIDTH].astype(BF16)
    wit_ref[...] = ht[T_WI:T_WI + IDX_HEADS] * (IDX_HEADS ** -0.5 * IDX_DIM ** -0.5)

    cr = cosr_ref[...]
    sr = sinr_ref[...]
    lane = lax.broadcasted_iota(I32, cr.shape, 1)
    first_half = (lane % 64) < ROPE_HALF

    def rope_rows(v):
        fwd = pltpu.roll(v, LANES - ROPE_HALF, 1)
        bwd = pltpu.roll(v, ROPE_HALF, 1)
        return v * cr + jnp.where(first_half, fwd, bwd) * sr

    for g in range(A_WIDTH // LANES):
        c0 = R_K + g * LANES
        hk = _dot(xb, wr_ref[:, c0:c0 + LANES])
        k_ref[:, g * LANES:(g + 1) * LANES] = rope_rows(hk).astype(BF16)
    hki = _dot(xb, wr_ref[:, R_KI:R_KI + LANES])
    ki_ref[...] = rope_rows(hki)[:, :IDX_DIM].astype(BF16)

    sag_ref[...] = _silu(_dot(xb, wr_ref[:, R_AG:R_AG + A_WIDTH])).astype(BF16)
    gv = _dot(xb, wr_ref[:, R_GV:R_GV + B_WIDTH])
    gg = _dot(xb, wr_ref[:, R_GG:R_GG + B_WIDTH])
    glu_ref[...] = gv * _sigmoid(gg)
    sbg_ref[...] = _silu(_dot(xb, wr_ref[:, R_BG:R_BG + B_WIDTH])).astype(BF16)
    cq_ref[...] = _dot(xb, wr_ref[:, R_CQ:R_CQ + C_KEYS]) * (C_KEY_DIM ** -0.5)
    ck_ref[...] = _dot(xb, wr_ref[:, R_CK:R_CK + C_KEYS])
    cv_ref[...] = _dot(xb, wr_ref[:, R_CV:R_CV + C_WIDTH])
    scg_ref[...] = _silu(_dot(xb, wr_ref[:, R_CG:R_CG + C_WIDTH])).astype(BF16)

    lr = _dot(xb, wr_ref[:, R_LR:R_LR + LANES])
    lr_h = lr.astype(BF16)
    lr_l = (lr - lr_h.astype(F32)).astype(BF16)
    z = (_dot(lr_h, w2h_ref[...]) + _dot(lr_l, w2h_ref[...]) + _dot(lr_h, w2l_ref[...])
         + gb_ref[...])
    log_sig = jnp.minimum(z, 0.0) - jnp.log(1.0 + jnp.exp(-jnp.abs(z)))
    la_ref[...] = log_sig * (1.0 / GATE_TAU)


def _inproj(x2, wt, wr, w2h, w2l, gb, tables):
    m, d = x2.shape
    tm = 512
    cost, sint, cosr, sinr = tables
    row = lambda w: pl.BlockSpec((tm, w), lambda i: (i, 0))
    col = lambda r: pl.BlockSpec((r, tm), lambda i: (0, i))
    full = lambda a: pl.BlockSpec(a.shape, lambda i: (0, 0))
    outs = [(col(A_WIDTH), (A_WIDTH, m), BF16),
            (col(A_WIDTH), (A_WIDTH, m), BF16),
            (col(A_WIDTH), (A_WIDTH, m), BF16),
            (col(IDX_HEADS), (IDX_HEADS, m), F32),
            (row(A_WIDTH), (m, A_WIDTH), BF16),
            (row(IDX_DIM), (m, IDX_DIM), BF16),
            (row(A_WIDTH), (m, A_WIDTH), BF16),
            (row(B_WIDTH), (m, B_WIDTH), F32),
            (row(B_WIDTH), (m, B_WIDTH), BF16),
            (row(C_KEYS), (m, C_KEYS), F32),
            (row(C_KEYS), (m, C_KEYS), F32),
            (row(C_WIDTH), (m, C_WIDTH), F32),
            (row(C_WIDTH), (m, C_WIDTH), BF16),
            (row(C_KEYS), (m, C_KEYS), F32)]
    return pl.pallas_call(
        _inproj_kernel,
        grid=(m // tm,),
        in_specs=[row(d), full(wt), full(wr), full(w2h), full(w2l), full(gb),
                  col(ROPE_HALF), col(ROPE_HALF), row(LANES), row(LANES)],
        out_specs=[o[0] for o in outs],
        out_shape=[jax.ShapeDtypeStruct(o[1], o[2]) for o in outs],
        compiler_params=pltpu.CompilerParams(
            dimension_semantics=("arbitrary",), vmem_limit_bytes=VMEM_LIMIT),
        name="in_proj",
    )(x2, wt, wr, w2h, w2l, gb, cost, sint, cosr, sinr)


KEY_LOWEST_FINITE = -2139095040
ATTN_TQ = 256
ATTN_TK = 256


def _key_to_float(key):
    return pltpu.bitcast(key ^ ((key >> 31) & 0x7FFFFFFF), F32)


def _attn_kernel(qt_ref, qit_ref, wit_ref, sag_ref, k_ref, ki_ref, vt_ref,
                 o_ref, sc_ref, hi_ref, qm_ref, acc_ref, tie_ref, s_ref, p_ref, *, tq, tk, topk, seq):
    i = pl.program_id(1)
    nchunks = (i + 1) * (tq // tk)
    rel = (lax.broadcasted_iota(I32, (tk, tq), 0) - lax.broadcasted_iota(I32, (tk, tq), 1))

    def score_pair(t, carry):
        js = (2 * t, 2 * t + 1)
        starts = [pl.multiple_of(j * tk, tk) for j in js]
        kics = [ki_ref[pl.ds(ks, tk), :] for ks in starts]
        scs = [jnp.zeros((tk, tq), F32) for _ in js]
        for h in range(IDX_HEADS):
            for n in range(2):
                lg = _dot(kics[n], qit_ref[h * IDX_DIM:(h + 1) * IDX_DIM, :])
                scs[n] = scs[n] + wit_ref[h:h + 1, :] * jnp.maximum(lg, 0.0)
        for n in range(2):
            sc = jnp.where(rel <= i * tq - js[n] * tk, scs[n], -jnp.inf)
            sc_ref[pl.ds(starts[n], tk), :] = sc
            upper = pltpu.bitcast(pltpu.bitcast(sc, I32) & -65536, F32)
            hi_ref[pl.ds(starts[n], tk), :] = upper.astype(BF16)
        return carry

    lax.fori_loop(0, (nchunks + 1) // 2, score_pair, 0)

    packed_rows = 16
    one16 = jnp.ones((tk, tq), BF16)
    zero16 = jnp.zeros((tk, tq), BF16)

    def count_upper(cand):
        def body(t, acc):
            for j in (2 * t, 2 * t + 1):
                ks = pl.multiple_of(j * tk, tk)
                hit = jnp.where(hi_ref[pl.ds(ks, tk), :] >= cand, one16, zero16)
                hit = hit.reshape(tk // (packed_rows * REDUCE_WAYS), REDUCE_WAYS, packed_rows, tq)
                part = hit[0]
                for r in range(1, hit.shape[0]):
                    part = part + hit[r]
                acc = acc + part.astype(F32)
            return acc
        acc = lax.fori_loop(0, (nchunks + 1) // 2, body,
                            jnp.zeros((REDUCE_WAYS, packed_rows, tq), F32))
        return acc.sum(axis=0).sum(axis=0, keepdims=True).astype(I32)

    def upper_key_to_bf16(key16):
        pattern = (key16 ^ ((key16 >> 31) & 0x7FFF)) & 0xFFFF
        return pltpu.bitcast(jnp.left_shift(pattern, 16), F32).astype(BF16)

    def count_hits(hits):
        def body(t, acc):
            for j in (2 * t, 2 * t + 1):
                ks = pl.multiple_of(j * tk, tk)
                acc = acc + _col_partial(hits(sc_ref[pl.ds(ks, tk), :]), jnp.sum)
            return acc
        acc = lax.fori_loop(0, (nchunks + 1) // 2, body, jnp.zeros((REDUCE_WAYS, 8, tq), I32))
        return acc.sum(axis=0).sum(axis=0, keepdims=True)

    def count_ge(cand):
        return count_hits(lambda s: jnp.where(s >= cand, 1, 0))

    zero = jnp.zeros((1, tq), I32)
    cnt0 = count_upper(jnp.zeros((1, tq), BF16))
    tau = jnp.where(cnt0 >= topk, zero, -32768)
    cnt_tau = jnp.where(cnt0 >= topk, cnt0, zero)

    def upper_step(it, carry):
        tau, cnt_tau = carry
        cand = tau + jnp.left_shift(jnp.int32(1), 14 - it)
        cnt = count_upper(upper_key_to_bf16(cand))
        take = cnt >= topk
        return jnp.where(take, cand, tau), jnp.where(take, cnt, cnt_tau)

    tau, cnt_tau = lax.fori_loop(0, 15, upper_step, (tau, cnt_tau))
    tau = jnp.left_shift(tau, 16)

    maybe_zero = (tau == 0) & (cnt_tau > topk)
    n_positive = lax.cond(
        jnp.max(jnp.where(maybe_zero, 1, 0)) > 0,
        lambda: count_hits(lambda s: jnp.where(s > 0.0, 1, 0)),
        lambda: jnp.full((1, tq), topk, I32))
    zero_thr = maybe_zero & (n_positive < topk)

    def unsettled(cnt_tau):
        settled = (cnt_tau == topk) | (cnt_tau == 0) | zero_thr
        return jnp.max(jnp.where(settled, 0, 1))

    def lower_cond(state):
        it, left, _, _ = state
        return (it < 16) & (left > 0)

    def lower_step(state):
        it, _, tau, cnt_tau = state
        for bit in (15 - it, 14 - it):
            cand = tau + jnp.left_shift(jnp.int32(1), bit)
            cnt = count_ge(_key_to_float(cand))
            take = cnt >= topk
            tau, cnt_tau = jnp.where(take, cand, tau), jnp.where(take, cnt, cnt_tau)
        return it + 2, unsettled(cnt_tau), tau, cnt_tau

    _, _, tau, cnt_tau = lax.while_loop(
        lower_cond, lower_step, (jnp.int32(0), unsettled(cnt_tau), tau, cnt_tau))
    thr = _key_to_float(jnp.maximum(tau, KEY_LOWEST_FINITE))

    has_ties = jnp.max(cnt_tau) > topk

    @pl.when(jnp.logical_not(has_ties))
    def _():
        def bias_chunk(j, carry):
            ks = pl.multiple_of(j * tk, tk)
            sc_ref[pl.ds(ks, tk), :] = jnp.where(sc_ref[pl.ds(ks, tk), :] >= thr, 0.0, NEG_BIG)
            return carry

        lax.fori_loop(0, nchunks, bias_chunk, 0)

    @pl.when(has_ties)
    def _():
        npairs = (nchunks + 1) // 2

        def mark(t, total):
            for j in (2 * t, 2 * t + 1):
                ks = pl.multiple_of(j * tk, tk)
                tie = jnp.where(sc_ref[pl.ds(ks, tk), :] == thr, 1.0, 0.0)
                hi_ref[pl.ds(ks, tk), :] = tie.astype(BF16)
                part = _col_partial(tie, jnp.sum).sum(axis=0)
                tie_ref[j] = part
                total = total + part
            return total

        n_tie = lax.fori_loop(0, npairs, mark, jnp.zeros((8, tq), F32)).sum(axis=0, keepdims=True)
        need = (topk - cnt_tau).astype(F32) + n_tie
        r_i = lax.broadcasted_iota(I32, (tk, tk), 0)
        c_i = lax.broadcasted_iota(I32, (tk, tk), 1)
        upto = jnp.where(c_i <= r_i, 1.0, 0.0).astype(BF16)

        def bias_pair(t, before):
            starts = [pl.multiple_of(j * tk, tk) for j in (2 * t, 2 * t + 1)]
            inside = [_dot(upto, hi_ref[pl.ds(ks, tk), :]) for ks in starts]
            for n, ks in enumerate(starts):
                s = sc_ref[pl.ds(ks, tk), :]
                tied = jnp.where(before + inside[n] <= need, 0.0, NEG_BIG)
                sc_ref[pl.ds(ks, tk), :] = jnp.where(s > thr, 0.0,
                                                     jnp.where(s == thr, tied, NEG_BIG))
                before = before + tie_ref[2 * t + n].sum(axis=0, keepdims=True)
            return before

        lax.fori_loop(0, npairs, bias_pair, jnp.zeros((1, tq), F32))

    head_row = lax.broadcasted_iota(I32, (LANES, tq), 0) // A_HEAD_DIM
    for h in range(A_HEADS):
        qpair = qt_ref[(h // 2) * LANES:(h // 2 + 1) * LANES, :]
        qm_ref[h] = jnp.where(head_row == (h % 2), qpair, jnp.zeros_like(qpair))
    acc_ref[...] = jnp.zeros(acc_ref.shape, F32)

    last_chunk = seq // tk - 1

    @pl.when((nchunks % 2 == 1) & (nchunks <= last_chunk))
    def _():
        sc_ref[pl.ds(pl.multiple_of(nchunks * tk, tk), tk), :] = jnp.full((tk, tq), NEG_BIG, F32)

    def step(j, slot, m_run, l_run, top_cur, alpha_prev):
        other = 1 - slot
        ks_a = pl.multiple_of(jnp.minimum(j + 1, last_chunk) * tk, tk)
        ks_c = pl.multiple_of(jnp.maximum(j - 1, 0) * tk, tk)
        m_new = jnp.maximum(m_run, top_cur)
        alpha = jnp.exp2(m_run - m_new)
        tops, sums = [], []
        for h in range(A_HEADS):
            hs = slice(h * A_HEAD_DIM, (h + 1) * A_HEAD_DIM)
            kc = k_ref[pl.ds(ks_a, tk), (h // 2) * LANES:(h // 2 + 1) * LANES]
            s = _dot(kc, qm_ref[h]) + sc_ref[pl.ds(ks_a, tk), :]
            s_ref[other * A_HEADS + h] = s
            tops.append(_col_reduce(s, jnp.max))
            pv = _dot(vt_ref[hs, pl.ds(ks_c, tk)], p_ref[other * A_HEADS + h])
            acc_ref[hs, :] = alpha_prev[h:h + 1, :] * acc_ref[hs, :] + pv
            p = jnp.exp2(s_ref[slot * A_HEADS + h] - m_new[h:h + 1, :])
            p_ref[slot * A_HEADS + h] = p.astype(BF16)
            sums.append(_col_reduce(p, jnp.sum))
        l_new = alpha * l_run + jnp.concatenate(sums, axis=0)
        return m_new, l_new, jnp.concatenate(tops, axis=0), alpha

    def attn_pair(t, carry):
        carry = step(2 * t, 0, *carry)
        return step(2 * t + 1, 1, *carry)

    tops0 = []
    for h in range(A_HEADS):
        s = _dot(k_ref[0:tk, (h // 2) * LANES:(h // 2 + 1) * LANES], qm_ref[h]) + sc_ref[0:tk, :]
        s_ref[h] = s
        tops0.append(_col_reduce(s, jnp.max))
    p_ref[A_HEADS:2 * A_HEADS] = jnp.zeros((A_HEADS, tk, tq), BF16)
    init = (jnp.full((A_HEADS, tq), NEG_BIG, F32), jnp.zeros((A_HEADS, tq), F32),
            jnp.concatenate(tops0, axis=0), jnp.ones((A_HEADS, tq), F32))
    npairs = (nchunks + 1) // 2
    _, l_fin, _, alpha_last = lax.fori_loop(0, npairs, attn_pair, init)
    ks_last = pl.multiple_of(jnp.minimum(2 * npairs - 1, last_chunk) * tk, tk)
    for h in range(A_HEADS):
        hs = slice(h * A_HEAD_DIM, (h + 1) * A_HEAD_DIM)
        pv = _dot(vt_ref[hs, pl.ds(ks_last, tk)], p_ref[A_HEADS + h])
        acc_ref[hs, :] = alpha_last[h:h + 1, :] * acc_ref[hs, :] + pv

    for h in range(A_HEADS):
        hs = slice(h * A_HEAD_DIM, (h + 1) * A_HEAD_DIM)
        acc_ref[hs, :] = acc_ref[hs, :] / l_fin[h:h + 1, :]
    o_ref[...] = (acc_ref[...].T * sag_ref[...].astype(F32)).astype(BF16)


def _attention(qt, qit, wit, sag, k, ki, vt, batch, seq):
    tq, tk = ATTN_TQ, ATTN_TK
    assert seq % (2 * tk) == 0 and tq % tk == 0
    nq = seq // tq
    topk = min(TOPK_MAX, seq // 4)
    once = pl.Buffered(1)
    return pl.pallas_call(
        functools.partial(_attn_kernel, tq=tq, tk=tk, topk=topk, seq=seq),
        grid=(batch, nq),
        in_specs=[pl.BlockSpec((A_WIDTH, tq), lambda b, i: (0, b * nq + i)),
                  pl.BlockSpec((A_WIDTH, tq), lambda b, i: (0, b * nq + i)),
                  pl.BlockSpec((IDX_HEADS, tq), lambda b, i: (0, b * nq + i)),
                  pl.BlockSpec((tq, A_WIDTH), lambda b, i: (b * nq + i, 0)),
                  pl.BlockSpec((seq, A_WIDTH), lambda b, i: (b, 0), pipeline_mode=once),
                  pl.BlockSpec((seq, IDX_DIM), lambda b, i: (b, 0), pipeline_mode=once),
                  pl.BlockSpec((A_WIDTH, seq), lambda b, i: (0, b), pipeline_mode=once)],
        out_specs=pl.BlockSpec((tq, A_WIDTH), lambda b, i: (b * nq + i, 0)),
        out_shape=jax.ShapeDtypeStruct((batch * seq, A_WIDTH), BF16),
        scratch_shapes=[pltpu.VMEM((seq, tq), F32),
                        pltpu.VMEM((seq, tq), BF16),
                        pltpu.VMEM((A_HEADS, LANES, tq), BF16),
                        pltpu.VMEM((A_WIDTH, tq), F32),
                        pltpu.VMEM((seq // tk, SUBLANES, tq), F32),
                        pltpu.VMEM((2 * A_HEADS, tk, tq), F32),
                        pltpu.VMEM((2 * A_HEADS, tk, tq), BF16)],
        compiler_params=pltpu.CompilerParams(
            dimension_semantics=("arbitrary", "arbitrary"), vmem_limit_bytes=VMEM_LIMIT),
        name="dsa_attention",
    )(qt, qit, wit, sag, k, ki, vt)


CONV_HALO = 32
CONV_SUB = 64


def _conv_kernel(cur_ref, halo_ref, sbg_ref, cw_ref, cb_ref, lg_ref, lb_ref, pw_ref, pb_ref,
                 o_ref, buf_ref, *, tt):
    i = pl.program_id(1)
    halo = halo_ref[...]
    buf_ref[0, 0:CONV_HALO, :] = jnp.where(i == 0, jnp.zeros_like(halo), halo)
    buf_ref[0, CONV_HALO:CONV_HALO + tt, :] = cur_ref[...]
    lead = CONV_HALO - (CONV_WIDTH - 1)
    span = tt + CONV_HALO - SUBLANES
    for s in range(1, SUBLANES):
        buf_ref[s, 0:span, :] = buf_ref[0, s:s + span, :]
    for r in range(tt // CONV_SUB):
        acc = jnp.zeros((CONV_SUB, B_WIDTH), F32)
        for j in range(CONV_WIDTH):
            shift, base = (lead + j) % SUBLANES, (lead + j) // SUBLANES * SUBLANES
            start = r * CONV_SUB + base
            acc = acc + buf_ref[shift, start:start + CONV_SUB, :] * cw_ref[j:j + 1, :]
        hc = acc + cb_ref[...]
        mu = jnp.mean(hc, axis=-1, keepdims=True)
        var = jnp.mean(jnp.square(hc - mu), axis=-1, keepdims=True)
        hn = (hc - mu) * lax.rsqrt(var + EPS) * lg_ref[...] + lb_ref[...]
        act = _silu(hn).astype(BF16)
        yb = _dot(act, pw_ref[...]) + pb_ref[...]
        rows = slice(r * CONV_SUB, (r + 1) * CONV_SUB)
        o_ref[rows, :] = (yb * sbg_ref[rows, :].astype(F32)).astype(BF16)


def _conformer(glu, sbg, cw, cb, lg, lb, pw, pb, batch, seq):
    tt = 512
    nt = seq // tt
    per = tt // CONV_HALO
    vec = lambda a: pl.BlockSpec(a.shape, lambda b, i: (0, 0))
    return pl.pallas_call(
        functools.partial(_conv_kernel, tt=tt),
        grid=(batch, nt),
        in_specs=[pl.BlockSpec((tt, B_WIDTH), lambda b, i: (b * nt + i, 0)),
                  pl.BlockSpec((CONV_HALO, B_WIDTH),
                               lambda b, i: (jnp.maximum((b * nt + i) * per - 1, 0), 0)),
                  pl.BlockSpec((tt, B_WIDTH), lambda b, i: (b * nt + i, 0)),
                  vec(cw), vec(cb), vec(lg), vec(lb), vec(pw), vec(pb)],
        out_specs=pl.BlockSpec((tt, B_WIDTH), lambda b, i: (b * nt + i, 0)),
        out_shape=jax.ShapeDtypeStruct((batch * seq, B_WIDTH), BF16),
        scratch_shapes=[pltpu.VMEM((SUBLANES, CONV_HALO + tt, B_WIDTH), F32)],
        compiler_params=pltpu.CompilerParams(dimension_semantics=("arbitrary", "arbitrary")),
        name="conformer_conv",
    )(glu, glu, sbg, cw, cb, lg, lb, pw, pb)


def _gla_kernel(q_ref, k_ref, v_ref, g_ref, scg_ref, gn_ref, o_ref, st_ref, *, tg, nbatch):
    c = GLA_CHUNK
    nsub = c // GLA_SUB

    @pl.when(pl.program_id(0) == 0)
    def _():
        st_ref[...] = jnp.zeros_like(st_ref)

    ri = lax.broadcasted_iota(I32, (c, c), 0)
    ci = lax.broadcasted_iota(I32, (c, c), 1)
    tri = jnp.where(ri >= ci, 1.0, 0.0).astype(BF16)
    r2 = lax.broadcasted_iota(I32, (C_WIDTH, C_KEYS), 0)
    c2 = lax.broadcasted_iota(I32, (C_WIDTH, C_KEYS), 1)
    key_blk = (r2 // C_VAL_DIM) == (c2 // C_KEY_DIM)
    key_blk_sub = jnp.concatenate([key_blk] * nsub, axis=1)
    r3 = lax.broadcasted_iota(I32, (C_WIDTH, C_WIDTH), 0)
    c3 = lax.broadcasted_iota(I32, (C_WIDTH, C_WIDTH), 1)
    val_blk = (r3 // C_VAL_DIM) == (c3 // C_VAL_DIM)
    ones_blk = jnp.where(val_blk, 1.0, 0.0).astype(BF16)
    r4 = lax.broadcasted_iota(I32, (c, C_WIDTH), 0)
    c4 = lax.broadcasted_iota(I32, (c, C_WIDTH), 1)
    causal = (c4 % c) <= r4
    sub_of_row = lax.broadcasted_iota(I32, (c, C_KEYS), 0) // GLA_SUB

    def one_batch(bi, rows):
        q = q_ref[bi, rows, :]
        k = k_ref[bi, rows, :]
        v = v_ref[bi, rows, :]
        g1, g2, g3 = _split3(g_ref[bi, rows, :])
        b = _dot(tri, g1) + _dot(tri, g2) + _dot(tri, g3)
        b_last = b[c - 1:c, :]
        qe = (q * jnp.exp(b)).astype(BF16)
        kd = (k * jnp.exp(b_last - b)).astype(BF16)
        st = st_ref[bi]
        yield

        heads = [b[s * GLA_SUB:s * GLA_SUB + 1, :] for s in range(nsub)]
        own = heads[0]
        for s in range(1, nsub):
            own = jnp.where(sub_of_row >= s, heads[s], own)
        qx = q * jnp.exp(b - own)
        q_cat = jnp.concatenate([jnp.where(sub_of_row == s, qx, 0.0) for s in range(nsub)],
                                axis=1).astype(BF16)
        k_cat = jnp.concatenate(
            [jnp.where(sub_of_row <= s, k * jnp.exp(jnp.minimum(heads[s] - b, GLA_EXP_CLAMP)), 0.0)
             for s in range(nsub)], axis=1)
        k_bd = jnp.where(key_blk_sub, jnp.concatenate([k_cat] * C_HEADS, axis=0), 0.0)
        yield
        a = jnp.where(causal, _dot_nt(q_cat, k_bd.astype(BF16)), 0.0)
        v_bd = jnp.where(val_blk, jnp.concatenate([v] * C_HEADS, axis=0), 0.0).astype(BF16)
        o = _dot(a.astype(BF16), v_bd) + _dot_nt(qe, st.astype(BF16))

        yield
        upd = _dot(v.T.astype(BF16), kd)
        st_ref[bi] = st * jnp.exp(b_last) + jnp.where(key_blk, upd, 0.0)

        yield
        oh, ol, _ = _split3(o * o)
        ms = (_dot(oh, ones_blk) + _dot(ol, ones_blk)) * (1.0 / C_VAL_DIM)
        on = o * lax.rsqrt(ms + EPS) * gn_ref[...]
        o_ref[bi, rows, :] = (on * scg_ref[bi, rows, :].astype(F32)).astype(BF16)

    def chunk(n, carry):
        rows = pl.ds(pl.multiple_of(n * c, c), c)
        pending = [one_batch(bi, rows) for bi in range(nbatch)]
        while pending:
            pending = [gen for gen in pending if next(gen, StopIteration) is not StopIteration]
        return carry

    lax.fori_loop(0, tg // c, chunk, 0)


def _gla(cq, ck, cv, la, scg, gn, batch, seq):
    tg = 512
    blk = lambda w: pl.BlockSpec((batch, tg, w), lambda i: (0, i, 0))
    by_batch = lambda a: a.reshape(batch, seq, a.shape[-1])
    out = pl.pallas_call(
        functools.partial(_gla_kernel, tg=tg, nbatch=batch),
        grid=(seq // tg,),
        in_specs=[blk(C_KEYS), blk(C_KEYS), blk(C_WIDTH), blk(C_KEYS), blk(C_WIDTH),
                  pl.BlockSpec(gn.shape, lambda i: (0, 0))],
        out_specs=blk(C_WIDTH),
        out_shape=jax.ShapeDtypeStruct((batch, seq, C_WIDTH), BF16),
        scratch_shapes=[pltpu.VMEM((batch, C_WIDTH, C_KEYS), F32)],
        compiler_params=pltpu.CompilerParams(dimension_semantics=("arbitrary",)),
        name="gla",
    )(by_batch(cq), by_batch(ck), by_batch(cv), by_batch(la), by_batch(scg), gn)
    return out.reshape(batch * seq, C_WIDTH)


def _outproj_kernel(ya_ref, yb_ref, yc_ref, x_ref, wo_ref, g_ref, b_ref, o_ref, *, alpha):
    y = (_dot(ya_ref[...], wo_ref[0:A_WIDTH, :])
         + _dot(yb_ref[...], wo_ref[A_WIDTH:A_WIDTH + B_WIDTH, :])
         + _dot(yc_ref[...], wo_ref[A_WIDTH + B_WIDTH:, :]))
    z = alpha * x_ref[...] + y
    mu = jnp.mean(z, axis=-1, keepdims=True)
    var = jnp.mean(jnp.square(z - mu), axis=-1, keepdims=True)
    o_ref[...] = (z - mu) * lax.rsqrt(var + EPS) * g_ref[...] + b_ref[...]


def _outproj(ya, yb, yc, x2, wo, g, b, alpha):
    m, d = x2.shape
    tm = 512
    row = lambda w: pl.BlockSpec((tm, w), lambda i: (i, 0))
    full = lambda a: pl.BlockSpec(a.shape, lambda i: (0, 0))
    return pl.pallas_call(
        functools.partial(_outproj_kernel, alpha=alpha),
        grid=(m // tm,),
        in_specs=[row(A_WIDTH), row(B_WIDTH), row(C_WIDTH), row(d), full(wo), full(g), full(b)],
        out_specs=row(d),
        out_shape=jax.ShapeDtypeStruct((m, d), F32),
        compiler_params=pltpu.CompilerParams(dimension_semantics=("arbitrary",)),
        name="out_proj",
    )(ya, yb, yc, x2, wo, g, b)


def _pack_in_weights(w):
    sizes = (A_WIDTH, A_WIDTH, A_WIDTH, A_WIDTH, IDX_HEADS * IDX_DIM, IDX_DIM, IDX_HEADS,
             2 * B_WIDTH, B_WIDTH, C_KEYS, C_KEYS, C_WIDTH, C_WIDTH, GATE_RANK)
    offs = np.concatenate([[0], np.cumsum(sizes)])
    (a_q, a_k, a_v, a_g, i_q, i_k, i_w, b_glu, b_g, c_q, c_k, c_v, c_g, c_lr) = [
        w[:, offs[n]:offs[n + 1]] for n in range(len(sizes))]
    d = w.shape[0]
    pad = lambda a, n: jnp.concatenate([a, jnp.zeros((d, n - a.shape[1]), a.dtype)], axis=1)
    wt = jnp.concatenate([a_q, i_q, a_v, pad(i_w, T_END - T_WI)], axis=1).T
    wr = jnp.concatenate([a_k, pad(i_k, LANES), a_g, b_glu, b_g, c_q, c_k, c_v, c_g,
                          pad(c_lr, LANES)], axis=1)
    return wt.astype(BF16), wr.astype(BF16)


def kernel(x, positions, w_in, conv_w, conv_b, cln_g, cln_b, pw_w, pw_b, gate_w2, gate_b,
           gnorm_g, w_out, ln_g, ln_b):
    batch, seq, d = x.shape
    depth = w_in.shape[0]
    alpha = float((2 * depth) ** 0.25)
    m = batch * seq
    tables = _rope_tables(positions)
    x2 = x.reshape(m, d)
    for l in range(depth):
        wt, wr = _pack_in_weights(w_in[l])
        w2 = jnp.zeros((LANES, C_KEYS), F32).at[:GATE_RANK].set(gate_w2[l])
        w2h = w2.astype(BF16)
        w2l = (w2 - w2h.astype(F32)).astype(BF16)
        (qt, qit, vt, wit, k, ki, sag, glu, sbg, cq, ck, cv, scg, la) = _inproj(
            x2, wt, wr, w2h, w2l, gate_b[l].reshape(1, C_KEYS), tables)
        ya = _attention(qt, qit, wit, sag, k, ki, vt, batch, seq)
        yb = _conformer(glu, sbg, conv_w[l], conv_b[l].reshape(1, -1), cln_g[l].reshape(1, -1),
                        cln_b[l].reshape(1, -1), pw_w[l].astype(BF16), pw_b[l].reshape(1, -1),
                        batch, seq)
        yc = _gla(cq, ck, cv, la, scg, gnorm_g[l].reshape(1, -1), batch, seq)
        x2 = _outproj(ya, yb, yc, x2, w_out[l].astype(BF16), ln_g[l].reshape(1, -1),
                      ln_b[l].reshape(1, -1), alpha)
    return x2.reshape(batch, seq, d)
```

```python
import functools

import numpy as np
import jax
import jax.numpy as jnp
from jax import lax
from jax.experimental import pallas as pl
from jax.experimental.pallas import tpu as pltpu

F32 = jnp.float32
BF16 = jnp.bfloat16
I32 = jnp.int32

A_HEADS = 8
A_HEAD_DIM = 64
A_WIDTH = A_HEADS * A_HEAD_DIM
IDX_HEADS = 8
IDX_DIM = 64
TOPK_MAX = 256
B_WIDTH = 256
CONV_WIDTH = 31
C_HEADS = 4
C_KEY_DIM = 32
C_VAL_DIM = 64
C_KEYS = C_HEADS * C_KEY_DIM
C_WIDTH = C_HEADS * C_VAL_DIM
GATE_RANK = 16
GATE_TAU = 16.0
GLA_CHUNK = 64
GLA_SUB = 16
GLA_EXP_CLAMP = 80.0
ROPE_THETA = 10000.0
ROPE_HALF = 32
EPS = 1e-5

LANES = 128
SUBLANES = 8
VMEM_LIMIT = 56 * 1024 * 1024
NEG_BIG = -1e30
Q_SCALE = A_HEAD_DIM ** -0.5 * float(np.log2(np.e))

R_K, R_KI, R_AG, R_GV, R_GG, R_BG, R_CQ, R_CK, R_CV, R_CG, R_LR, R_END = (
    0, 512, 640, 1152, 1408, 1664, 1920, 2048, 2176, 2432, 2688, 2816)
T_Q, T_QI, T_V, T_WI, T_END = 0, 512, 1024, 1536, 1552


def _dot(a, b):
    return jnp.dot(a, b, preferred_element_type=F32)


def _dot_nt(a, b):
    return lax.dot_general(a, b, (((1,), (1,)), ((), ())), preferred_element_type=F32)


def _sigmoid(x):
    return 1.0 / (1.0 + jnp.exp(-x))


def _silu(x):
    return x * _sigmoid(x)


REDUCE_WAYS = 4


def _col_partial(x, op):
    r, c = x.shape
    return op(x.reshape(r // (8 * REDUCE_WAYS), REDUCE_WAYS, 8, c), axis=0)


def _col_reduce(x, op):
    return op(op(_col_partial(x, op), axis=0), axis=0, keepdims=True)


def _split3(x):
    a = x.astype(BF16)
    r = x - a.astype(F32)
    b = r.astype(BF16)
    c = (r - b.astype(F32)).astype(BF16)
    return a, b, c


def _rope_kernel(post_ref, posc_ref, invc_ref, invr_ref, sgn_ref,
                 cost_ref, sint_ref, cosr_ref, sinr_ref):
    ang_t = invc_ref[...] * post_ref[...]
    cost_ref[...] = jnp.cos(ang_t)
    sint_ref[...] = jnp.sin(ang_t)
    ang_r = posc_ref[...] * invr_ref[...]
    cosr_ref[...] = jnp.cos(ang_r)
    sinr_ref[...] = jnp.sin(ang_r) * sgn_ref[...]


def _rope_tables(positions):
    m = positions.size
    tm = 1024
    pos = positions.reshape(-1).astype(F32)
    inv = ROPE_THETA ** (-jnp.arange(0, 2 * ROPE_HALF, 2, dtype=F32) / (2 * ROPE_HALF))
    lane = np.arange(LANES)
    sgn = jnp.asarray(np.where(lane % 64 < ROPE_HALF, -1.0, 1.0), F32).reshape(1, LANES)
    inv_r = jnp.tile(inv, LANES // ROPE_HALF).reshape(1, LANES)
    return pl.pallas_call(
        _rope_kernel,
        grid=(m // tm,),
        in_specs=[pl.BlockSpec((1, tm), lambda i: (0, i)),
                  pl.BlockSpec((tm, 1), lambda i: (i, 0)),
                  pl.BlockSpec((ROPE_HALF, 1), lambda i: (0, 0)),
                  pl.BlockSpec((1, LANES), lambda i: (0, 0)),
                  pl.BlockSpec((1, LANES), lambda i: (0, 0))],
        out_specs=[pl.BlockSpec((ROPE_HALF, tm), lambda i: (0, i)),
                   pl.BlockSpec((ROPE_HALF, tm), lambda i: (0, i)),
                   pl.BlockSpec((tm, LANES), lambda i: (i, 0)),
                   pl.BlockSpec((tm, LANES), lambda i: (i, 0))],
        out_shape=[jax.ShapeDtypeStruct((ROPE_HALF, m), F32),
                   jax.ShapeDtypeStruct((ROPE_HALF, m), F32),
                   jax.ShapeDtypeStruct((m, LANES), F32),
                   jax.ShapeDtypeStruct((m, LANES), F32)],
        name="rope_tables",
    )(pos.reshape(1, m), pos.reshape(m, 1), inv.reshape(ROPE_HALF, 1), inv_r, sgn)


def _inproj_kernel(x_ref, wt_ref, wr_ref, w2h_ref, w2l_ref, gb_ref,
                   cost_ref, sint_ref, cosr_ref, sinr_ref,
                   qt_ref, qit_ref, vt_ref, wit_ref, k_ref, ki_ref, sag_ref,
                   glu_ref, sbg_ref, cq_ref, ck_ref, cv_ref, scg_ref, la_ref):
    xb = x_ref[...].astype(BF16)
    ct = cost_ref[...]
    st = sint_ref[...]

    ht = _dot_nt(wt_ref[...], xb)
    for h in range(A_HEADS):
        for base, out, scale in ((T_Q, qt_ref, Q_SCALE), (T_QI, qit_ref, 1.0)):
            r0 = base + h * 64
            x1 = ht[r0:r0 + 32]
            x2 = ht[r0 + 32:r0 + 64]
            o0 = h * 64
            out[o0:o0 + 32, :] = ((x1 * ct - x2 * st) * scale).astype(BF16)
            out[o0 + 32:o0 + 64, :] = ((x2 * ct + x1 * st) * scale).astype(BF16)
    vt_ref[...] = ht[T_V:T_V + A_WIDTH].astype(BF16)
    wit_ref[...] = ht[T_WI:T_WI + IDX_HEADS] * (IDX_HEADS ** -0.5 * IDX_DIM ** -0.5)

    cr = cosr_ref[...]
    sr = sinr_ref[...]
    lane = lax.broadcasted_iota(I32, cr.shape, 1)
    first_half = (lane % 64) < ROPE_HALF

    def rope_rows(v):
        fwd = pltpu.roll(v, LANES - ROPE_HALF, 1)
        bwd = pltpu.roll(v, ROPE_HALF, 1)
        return v * cr + jnp.where(first_half, fwd, bwd) * sr

    for g in range(A_WIDTH // LANES):
        c0 = R_K + g * LANES
        hk = _dot(xb, wr_ref[:, c0:c0 + LANES])
        k_ref[:, g * LANES:(g + 1) * LANES] = rope_rows(hk).astype(BF16)
    hki = _dot(xb, wr_ref[:, R_KI:R_KI + LANES])
    ki_ref[...] = rope_rows(hki)[:, :IDX_DIM].astype(BF16)

    sag_ref[...] = _silu(_dot(xb, wr_ref[:, R_AG:R_AG + A_WIDTH])).astype(BF16)
    gv = _dot(xb, wr_ref[:, R_GV:R_GV + B_WIDTH])
    gg = _dot(xb, wr_ref[:, R_GG:R_GG + B_WIDTH])
    glu_ref[...] = gv * _sigmoid(gg)
    sbg_ref[...] = _silu(_dot(xb, wr_ref[:, R_BG:R_BG + B_WIDTH])).astype(BF16)
    cq_ref[...] = _dot(xb, wr_ref[:, R_CQ:R_CQ + C_KEYS]) * (C_KEY_DIM ** -0.5)
    ck_ref[...] = _dot(xb, wr_ref[:, R_CK:R_CK + C_KEYS])
    cv_ref[...] = _dot(xb, wr_ref[:, R_CV:R_CV + C_WIDTH])
    scg_ref[...] = _silu(_dot(xb, wr_ref[:, R_CG:R_CG + C_WIDTH])).astype(BF16)

    lr = _dot(xb, wr_ref[:, R_LR:R_LR + LANES])
    lr_h = lr.astype(BF16)
    lr_l = (lr - lr_h.astype(F32)).astype(BF16)
    z = (_dot(lr_h, w2h_ref[...]) + _dot(lr_l, w2h_ref[...]) + _dot(lr_h, w2l_ref[...])
         + gb_ref[...])
    log_sig = jnp.minimum(z, 0.0) - jnp.log(1.0 + jnp.exp(-jnp.abs(z)))
    la_ref[...] = log_sig * (1.0 / GATE_TAU)


def _inproj(x2, wt, wr, w2h, w2l, gb, tables):
    m, d = x2.shape
    tm = 512
    cost, sint, cosr, sinr = tables
    row = lambda w: pl.BlockSpec((tm, w), lambda i: (i, 0))
    col = lambda r: pl.BlockSpec((r, tm), lambda i: (0, i))
    full = lambda a: pl.BlockSpec(a.shape, lambda i: (0, 0))
    outs = [(col(A_WIDTH), (A_WIDTH, m), BF16),
            (col(A_WIDTH), (A_WIDTH, m), BF16),
            (col(A_WIDTH), (A_WIDTH, m), BF16),
            (col(IDX_HEADS), (IDX_HEADS, m), F32),
            (row(A_WIDTH), (m, A_WIDTH), BF16),
            (row(IDX_DIM), (m, IDX_DIM), BF16),
            (row(A_WIDTH), (m, A_WIDTH), BF16),
            (row(B_WIDTH), (m, B_WIDTH), F32),
            (row(B_WIDTH), (m, B_WIDTH), BF16),
            (row(C_KEYS), (m, C_KEYS), F32),
            (row(C_KEYS), (m, C_KEYS), F32),
            (row(C_WIDTH), (m, C_WIDTH), F32),
            (row(C_WIDTH), (m, C_WIDTH), BF16),
            (row(C_KEYS), (m, C_KEYS), F32)]
    return pl.pallas_call(
        _inproj_kernel,
        grid=(m // tm,),
        in_specs=[row(d), full(wt), full(wr), full(w2h), full(w2l), full(gb),
                  col(ROPE_HALF), col(ROPE_HALF), row(LANES), row(LANES)],
        out_specs=[o[0] for o in outs],
        out_shape=[jax.ShapeDtypeStruct(o[1], o[2]) for o in outs],
        compiler_params=pltpu.CompilerParams(
            dimension_semantics=("arbitrary",), vmem_limit_bytes=VMEM_LIMIT),
        name="in_proj",
    )(x2, wt, wr, w2h, w2l, gb, cost, sint, cosr, sinr)


KEY_LOWEST_FINITE = -2139095040
ATTN_TQ = 256
ATTN_TK = 256


def _key_to_float(key):
    return pltpu.bitcast(key ^ ((key >> 31) & 0x7FFFFFFF), F32)


def _attn_kernel(qt_ref, qit_ref, wit_ref, sag_ref, k_ref, ki_ref, vt_ref,
                 o_ref, sc_ref, hi_ref, qm_ref, acc_ref, tie_ref, s_ref, p_ref, *, tq, tk, topk, seq):
    i = pl.program_id(1)
    nchunks = (i + 1) * (tq // tk)
    rel = (lax.broadcasted_iota(I32, (tk, tq), 0) - lax.broadcasted_iota(I32, (tk, tq), 1))

    def score_pair(t, carry):
        js = (2 * t, 2 * t + 1)
        starts = [pl.multiple_of(j * tk, tk) for j in js]
        kics = [ki_ref[pl.ds(ks, tk), :] for ks in starts]
        scs = [jnp.zeros((tk, tq), F32) for _ in js]
        for h in range(IDX_HEADS):
            for n in range(2):
                lg = _dot(kics[n], qit_ref[h * IDX_DIM:(h + 1) * IDX_DIM, :])
                scs[n] = scs[n] + wit_ref[h:h + 1, :] * jnp.maximum(lg, 0.0)
        for n in range(2):
            sc = jnp.where(rel <= i * tq - js[n] * tk, scs[n], -jnp.inf)
            sc_ref[pl.ds(starts[n], tk), :] = sc
            upper = pltpu.bitcast(pltpu.bitcast(sc, I32) & -65536, F32)
            hi_ref[pl.ds(starts[n], tk), :] = upper.astype(BF16)
        return carry

    lax.fori_loop(0, (nchunks + 1) // 2, score_pair, 0)

    packed_rows = 16
    one16 = jnp.ones((tk, tq), BF16)
    zero16 = jnp.zeros((tk, tq), BF16)

    def count_upper(cand):
        def body(t, acc):
            for j in (2 * t, 2 * t + 1):
                ks = pl.multiple_of(j * tk, tk)
                hit = jnp.where(hi_ref[pl.ds(ks, tk), :] >= cand, one16, zero16)
                hit = hit.reshape(tk // (packed_rows * REDUCE_WAYS), REDUCE_WAYS, packed_rows, tq)
                part = hit[0]
                for r in range(1, hit.shape[0]):
                    part = part + hit[r]
                acc = acc + part.astype(F32)
            return acc
        acc = lax.fori_loop(0, (nchunks + 1) // 2, body,
                            jnp.zeros((REDUCE_WAYS, packed_rows, tq), F32))
        return acc.sum(axis=0).sum(axis=0, keepdims=True).astype(I32)

    def upper_key_to_bf16(key16):
        pattern = (key16 ^ ((key16 >> 31) & 0x7FFF)) & 0xFFFF
        return pltpu.bitcast(jnp.left_shift(pattern, 16), F32).astype(BF16)

    def count_ge(cand):
        def body(t, acc):
            for j in (2 * t, 2 * t + 1):
                ks = pl.multiple_of(j * tk, tk)
                hit = jnp.where(sc_ref[pl.ds(ks, tk), :] >= cand, 1, 0)
                acc = acc + _col_partial(hit, jnp.sum)
            return acc
        acc = lax.fori_loop(0, (nchunks + 1) // 2, body, jnp.zeros((REDUCE_WAYS, 8, tq), I32))
        return acc.sum(axis=0).sum(axis=0, keepdims=True)

    zero = jnp.zeros((1, tq), I32)
    cnt0 = count_upper(jnp.zeros((1, tq), BF16))
    tau = jnp.where(cnt0 >= topk, zero, -32768)
    cnt_tau = jnp.where(cnt0 >= topk, cnt0, zero)

    def upper_step(it, carry):
        tau, cnt_tau = carry
        cand = tau + jnp.left_shift(jnp.int32(1), 14 - it)
        cnt = count_upper(upper_key_to_bf16(cand))
        take = cnt >= topk
        return jnp.where(take, cand, tau), jnp.where(take, cnt, cnt_tau)

    tau, cnt_tau = lax.fori_loop(0, 15, upper_step, (tau, cnt_tau))
    tau = jnp.left_shift(tau, 16)

    def lower_step(it, carry):
        tau, cnt_tau = carry
        cand = tau + jnp.left_shift(jnp.int32(1), 15 - it)
        cnt = count_ge(_key_to_float(cand))
        take = cnt >= topk
        return jnp.where(take, cand, tau), jnp.where(take, cnt, cnt_tau)

    tau, cnt_tau = lax.fori_loop(0, 16, lower_step, (tau, cnt_tau))
    thr = _key_to_float(jnp.maximum(tau, KEY_LOWEST_FINITE))

    has_ties = jnp.max(cnt_tau) > topk

    @pl.when(jnp.logical_not(has_ties))
    def _():
        def bias_chunk(j, carry):
            ks = pl.multiple_of(j * tk, tk)
            sc_ref[pl.ds(ks, tk), :] = jnp.where(sc_ref[pl.ds(ks, tk), :] >= thr, 0.0, NEG_BIG)
            return carry

        lax.fori_loop(0, nchunks, bias_chunk, 0)

    @pl.when(has_ties)
    def _():
        npairs = (nchunks + 1) // 2

        def mark(t, total):
            for j in (2 * t, 2 * t + 1):
                ks = pl.multiple_of(j * tk, tk)
                tie = jnp.where(sc_ref[pl.ds(ks, tk), :] == thr, 1.0, 0.0)
                hi_ref[pl.ds(ks, tk), :] = tie.astype(BF16)
                part = _col_partial(tie, jnp.sum).sum(axis=0)
                tie_ref[j] = part
                total = total + part
            return total

        n_tie = lax.fori_loop(0, npairs, mark, jnp.zeros((8, tq), F32)).sum(axis=0, keepdims=True)
        need = (topk - cnt_tau).astype(F32) + n_tie
        r_i = lax.broadcasted_iota(I32, (tk, tk), 0)
        c_i = lax.broadcasted_iota(I32, (tk, tk), 1)
        upto = jnp.where(c_i <= r_i, 1.0, 0.0).astype(BF16)

        def bias_pair(t, before):
            starts = [pl.multiple_of(j * tk, tk) for j in (2 * t, 2 * t + 1)]
            inside = [_dot(upto, hi_ref[pl.ds(ks, tk), :]) for ks in starts]
            for n, ks in enumerate(starts):
                s = sc_ref[pl.ds(ks, tk), :]
                tied = jnp.where(before + inside[n] <= need, 0.0, NEG_BIG)
                sc_ref[pl.ds(ks, tk), :] = jnp.where(s > thr, 0.0,
                                                     jnp.where(s == thr, tied, NEG_BIG))
                before = before + tie_ref[2 * t + n].sum(axis=0, keepdims=True)
            return before

        lax.fori_loop(0, npairs, bias_pair, jnp.zeros((1, tq), F32))

    head_row = lax.broadcasted_iota(I32, (LANES, tq), 0) // A_HEAD_DIM
    for h in range(A_HEADS):
        qpair = qt_ref[(h // 2) * LANES:(h // 2 + 1) * LANES, :]
        qm_ref[h] = jnp.where(head_row == (h % 2), qpair, jnp.zeros_like(qpair))
    acc_ref[...] = jnp.zeros(acc_ref.shape, F32)

    last_chunk = seq // tk - 1

    @pl.when((nchunks % 2 == 1) & (nchunks <= last_chunk))
    def _():
        sc_ref[pl.ds(pl.multiple_of(nchunks * tk, tk), tk), :] = jnp.full((tk, tq), NEG_BIG, F32)

    def step(j, slot, m_run, l_run, top_cur, alpha_prev):
        other = 1 - slot
        ks_a = pl.multiple_of(jnp.minimum(j + 1, last_chunk) * tk, tk)
        ks_c = pl.multiple_of(jnp.maximum(j - 1, 0) * tk, tk)
        m_new = jnp.maximum(m_run, top_cur)
        alpha = jnp.exp2(m_run - m_new)
        tops, sums = [], []
        for h in range(A_HEADS):
            hs = slice(h * A_HEAD_DIM, (h + 1) * A_HEAD_DIM)
            kc = k_ref[pl.ds(ks_a, tk), (h // 2) * LANES:(h // 2 + 1) * LANES]
            s = _dot(kc, qm_ref[h]) + sc_ref[pl.ds(ks_a, tk), :]
            s_ref[other * A_HEADS + h] = s
            tops.append(_col_reduce(s, jnp.max))
            pv = _dot(vt_ref[hs, pl.ds(ks_c, tk)], p_ref[other * A_HEADS + h])
            acc_ref[hs, :] = alpha_prev[h:h + 1, :] * acc_ref[hs, :] + pv
            p = jnp.exp2(s_ref[slot * A_HEADS + h] - m_new[h:h + 1, :])
            p_ref[slot * A_HEADS + h] = p.astype(BF16)
            sums.append(_col_reduce(p, jnp.sum))
        l_new = alpha * l_run + jnp.concatenate(sums, axis=0)
        return m_new, l_new, jnp.concatenate(tops, axis=0), alpha

    def attn_pair(t, carry):
        carry = step(2 * t, 0, *carry)
        return step(2 * t + 1, 1, *carry)

    tops0 = []
    for h in range(A_HEADS):
        s = _dot(k_ref[0:tk, (h // 2) * LANES:(h // 2 + 1) * LANES], qm_ref[h]) + sc_ref[0:tk, :]
        s_ref[h] = s
        tops0.append(_col_reduce(s, jnp.max))
    p_ref[A_HEADS:2 * A_HEADS] = jnp.zeros((A_HEADS, tk, tq), BF16)
    init = (jnp.full((A_HEADS, tq), NEG_BIG, F32), jnp.zeros((A_HEADS, tq), F32),
            jnp.concatenate(tops0, axis=0), jnp.ones((A_HEADS, tq), F32))
    npairs = (nchunks + 1) // 2
    _, l_fin, _, alpha_last = lax.fori_loop(0, npairs, attn_pair, init)
    ks_last = pl.multiple_of(jnp.minimum(2 * npairs - 1, last_chunk) * tk, tk)
    for h in range(A_HEADS):
        hs = slice(h * A_HEAD_DIM, (h + 1) * A_HEAD_DIM)
        pv = _dot(vt_ref[hs, pl.ds(ks_last, tk)], p_ref[A_HEADS + h])
        acc_ref[hs, :] = alpha_last[h:h + 1, :] * acc_ref[hs, :] + pv

    for h in range(A_HEADS):
        hs = slice(h * A_HEAD_DIM, (h + 1) * A_HEAD_DIM)
        acc_ref[hs, :] = acc_ref[hs, :] / l_fin[h:h + 1, :]
    o_ref[...] = (acc_ref[...].T * sag_ref[...].astype(F32)).astype(BF16)


def _attention(qt, qit, wit, sag, k, ki, vt, batch, seq):
    tq, tk = ATTN_TQ, ATTN_TK
    assert seq % (2 * tk) == 0 and tq % tk == 0
    nq = seq // tq
    topk = min(TOPK_MAX, seq // 4)
    once = pl.Buffered(1)
    return pl.pallas_call(
        functools.partial(_attn_kernel, tq=tq, tk=tk, topk=topk, seq=seq),
        grid=(batch, nq),
        in_specs=[pl.BlockSpec((A_WIDTH, tq), lambda b, i: (0, b * nq + i)),
                  pl.BlockSpec((A_WIDTH, tq), lambda b, i: (0, b * nq + i)),
                  pl.BlockSpec((IDX_HEADS, tq), lambda b, i: (0, b * nq + i)),
                  pl.BlockSpec((tq, A_WIDTH), lambda b, i: (b * nq + i, 0)),
                  pl.BlockSpec((seq, A_WIDTH), lambda b, i: (b, 0), pipeline_mode=once),
                  pl.BlockSpec((seq, IDX_DIM), lambda b, i: (b, 0), pipeline_mode=once),
                  pl.BlockSpec((A_WIDTH, seq), lambda b, i: (0, b), pipeline_mode=once)],
        out_specs=pl.BlockSpec((tq, A_WIDTH), lambda b, i: (b * nq + i, 0)),
        out_shape=jax.ShapeDtypeStruct((batch * seq, A_WIDTH), BF16),
        scratch_shapes=[pltpu.VMEM((seq, tq), F32),
                        pltpu.VMEM((seq, tq), BF16),
                        pltpu.VMEM((A_HEADS, LANES, tq), BF16),
                        pltpu.VMEM((A_WIDTH, tq), F32),
                        pltpu.VMEM((seq // tk, SUBLANES, tq), F32),
                        pltpu.VMEM((2 * A_HEADS, tk, tq), F32),
                        pltpu.VMEM((2 * A_HEADS, tk, tq), BF16)],
        compiler_params=pltpu.CompilerParams(
            dimension_semantics=("arbitrary", "arbitrary"), vmem_limit_bytes=VMEM_LIMIT),
        name="dsa_attention",
    )(qt, qit, wit, sag, k, ki, vt)


CONV_HALO = 32
CONV_SUB = 64


def _conv_kernel(cur_ref, halo_ref, sbg_ref, cw_ref, cb_ref, lg_ref, lb_ref, pw_ref, pb_ref,
                 o_ref, buf_ref, *, tt):
    i = pl.program_id(1)
    halo = halo_ref[...]
    buf_ref[0, 0:CONV_HALO, :] = jnp.where(i == 0, jnp.zeros_like(halo), halo)
    buf_ref[0, CONV_HALO:CONV_HALO + tt, :] = cur_ref[...]
    lead = CONV_HALO - (CONV_WIDTH - 1)
    span = tt + CONV_HALO - SUBLANES
    for s in range(1, SUBLANES):
        buf_ref[s, 0:span, :] = buf_ref[0, s:s + span, :]
    for r in range(tt // CONV_SUB):
        acc = jnp.zeros((CONV_SUB, B_WIDTH), F32)
        for j in range(CONV_WIDTH):
            shift, base = (lead + j) % SUBLANES, (lead + j) // SUBLANES * SUBLANES
            start = r * CONV_SUB + base
            acc = acc + buf_ref[shift, start:start + CONV_SUB, :] * cw_ref[j:j + 1, :]
        hc = acc + cb_ref[...]
        mu = jnp.mean(hc, axis=-1, keepdims=True)
        var = jnp.mean(jnp.square(hc - mu), axis=-1, keepdims=True)
        hn = (hc - mu) * lax.rsqrt(var + EPS) * lg_ref[...] + lb_ref[...]
        act = _silu(hn).astype(BF16)
        yb = _dot(act, pw_ref[...]) + pb_ref[...]
        rows = slice(r * CONV_SUB, (r + 1) * CONV_SUB)
        o_ref[rows, :] = (yb * sbg_ref[rows, :].astype(F32)).astype(BF16)


def _conformer(glu, sbg, cw, cb, lg, lb, pw, pb, batch, seq):
    tt = 512
    nt = seq // tt
    per = tt // CONV_HALO
    vec = lambda a: pl.BlockSpec(a.shape, lambda b, i: (0, 0))
    return pl.pallas_call(
        functools.partial(_conv_kernel, tt=tt),
        grid=(batch, nt),
        in_specs=[pl.BlockSpec((tt, B_WIDTH), lambda b, i: (b * nt + i, 0)),
                  pl.BlockSpec((CONV_HALO, B_WIDTH),
                               lambda b, i: (jnp.maximum((b * nt + i) * per - 1, 0), 0)),
                  pl.BlockSpec((tt, B_WIDTH), lambda b, i: (b * nt + i, 0)),
                  vec(cw), vec(cb), vec(lg), vec(lb), vec(pw), vec(pb)],
        out_specs=pl.BlockSpec((tt, B_WIDTH), lambda b, i: (b * nt + i, 0)),
        out_shape=jax.ShapeDtypeStruct((batch * seq, B_WIDTH), BF16),
        scratch_shapes=[pltpu.VMEM((SUBLANES, CONV_HALO + tt, B_WIDTH), F32)],
        compiler_params=pltpu.CompilerParams(dimension_semantics=("arbitrary", "arbitrary")),
        name="conformer_conv",
    )(glu, glu, sbg, cw, cb, lg, lb, pw, pb)


def _gla_kernel(q_ref, k_ref, v_ref, g_ref, scg_ref, gn_ref, o_ref, st_ref, *, tg, nbatch):
    c = GLA_CHUNK
    nsub = c // GLA_SUB

    @pl.when(pl.program_id(0) == 0)
    def _():
        st_ref[...] = jnp.zeros_like(st_ref)

    ri = lax.broadcasted_iota(I32, (c, c), 0)
    ci = lax.broadcasted_iota(I32, (c, c), 1)
    tri = jnp.where(ri >= ci, 1.0, 0.0).astype(BF16)
    r2 = lax.broadcasted_iota(I32, (C_WIDTH, C_KEYS), 0)
    c2 = lax.broadcasted_iota(I32, (C_WIDTH, C_KEYS), 1)
    key_blk = (r2 // C_VAL_DIM) == (c2 // C_KEY_DIM)
    key_blk_sub = jnp.concatenate([key_blk] * nsub, axis=1)
    r3 = lax.broadcasted_iota(I32, (C_WIDTH, C_WIDTH), 0)
    c3 = lax.broadcasted_iota(I32, (C_WIDTH, C_WIDTH), 1)
    val_blk = (r3 // C_VAL_DIM) == (c3 // C_VAL_DIM)
    ones_blk = jnp.where(val_blk, 1.0, 0.0).astype(BF16)
    r4 = lax.broadcasted_iota(I32, (c, C_WIDTH), 0)
    c4 = lax.broadcasted_iota(I32, (c, C_WIDTH), 1)
    causal = (c4 % c) <= r4
    sub_of_row = lax.broadcasted_iota(I32, (c, C_KEYS), 0) // GLA_SUB

    def one_batch(bi, rows):
        q = q_ref[bi, rows, :]
        k = k_ref[bi, rows, :]
        v = v_ref[bi, rows, :]
        g1, g2, g3 = _split3(g_ref[bi, rows, :])
        b = _dot(tri, g1) + _dot(tri, g2) + _dot(tri, g3)
        yield
        b_last = b[c - 1:c, :]
        qe = (q * jnp.exp(b)).astype(BF16)
        kd = (k * jnp.exp(b_last - b)).astype(BF16)
        st = st_ref[bi]
        yield

        heads = [b[s * GLA_SUB:s * GLA_SUB + 1, :] for s in range(nsub)]
        own = heads[0]
        for s in range(1, nsub):
            own = jnp.where(sub_of_row >= s, heads[s], own)
        qx = q * jnp.exp(b - own)
        yield
        q_cat = jnp.concatenate([jnp.where(sub_of_row == s, qx, 0.0) for s in range(nsub)],
                                axis=1).astype(BF16)
        k_cat = jnp.concatenate(
            [jnp.where(sub_of_row <= s, k * jnp.exp(jnp.minimum(heads[s] - b, GLA_EXP_CLAMP)), 0.0)
             for s in range(nsub)], axis=1)
        yield
        k_bd = jnp.where(key_blk_sub, jnp.concatenate([k_cat] * C_HEADS, axis=0), 0.0)
        yield
        a = jnp.where(causal, _dot_nt(q_cat, k_bd.astype(BF16)), 0.0)
        yield
        v_bd = jnp.where(val_blk, jnp.concatenate([v] * C_HEADS, axis=0), 0.0).astype(BF16)
        o = _dot(a.astype(BF16), v_bd) + _dot_nt(qe, st.astype(BF16))

        yield
        upd = _dot(v.T.astype(BF16), kd)
        st_ref[bi] = st * jnp.exp(b_last) + jnp.where(key_blk, upd, 0.0)

        yield
        oh, ol, _ = _split3(o * o)
        ms = (_dot(oh, ones_blk) + _dot(ol, ones_blk)) * (1.0 / C_VAL_DIM)
        yield
        on = o * lax.rsqrt(ms + EPS) * gn_ref[...]
        o_ref[bi, rows, :] = (on * scg_ref[bi, rows, :].astype(F32)).astype(BF16)

    def chunk(n, carry):
        rows = pl.ds(pl.multiple_of(n * c, c), c)
        pending = [one_batch(bi, rows) for bi in range(nbatch)]
        while pending:
            pending = [gen for gen in pending if next(gen, StopIteration) is not StopIteration]
        return carry

    lax.fori_loop(0, tg // c, chunk, 0)


def _gla(cq, ck, cv, la, scg, gn, batch, seq):
    tg = 512
    blk = lambda w: pl.BlockSpec((batch, tg, w), lambda i: (0, i, 0))
    by_batch = lambda a: a.reshape(batch, seq, a.shape[-1])
    out = pl.pallas_call(
        functools.partial(_gla_kernel, tg=tg, nbatch=batch),
        grid=(seq // tg,),
        in_specs=[blk(C_KEYS), blk(C_KEYS), blk(C_WIDTH), blk(C_KEYS), blk(C_WIDTH),
                  pl.BlockSpec(gn.shape, lambda i: (0, 0))],
        out_specs=blk(C_WIDTH),
        out_shape=jax.ShapeDtypeStruct((batch, seq, C_WIDTH), BF16),
        scratch_shapes=[pltpu.VMEM((batch, C_WIDTH, C_KEYS), F32)],
        compiler_params=pltpu.CompilerParams(dimension_semantics=("arbitrary",)),
        name="gla",
    )(by_batch(cq), by_batch(ck), by_batch(cv), by_batch(la), by_batch(scg), gn)
    return out.reshape(batch * seq, C_WIDTH)


def _outproj_kernel(ya_ref, yb_ref, yc_ref, x_ref, wo_ref, g_ref, b_ref, o_ref, *, alpha):
    y = (_dot(ya_ref[...], wo_ref[0:A_WIDTH, :])
         + _dot(yb_ref[...], wo_ref[A_WIDTH:A_WIDTH + B_WIDTH, :])
         + _dot(yc_ref[...], wo_ref[A_WIDTH + B_WIDTH:, :]))
    z = alpha * x_ref[...] + y
    mu = jnp.mean(z, axis=-1, keepdims=True)
    var = jnp.mean(jnp.square(z - mu), axis=-1, keepdims=True)
    o_ref[...] = (z - mu) * lax.rsqrt(var + EPS) * g_ref[...] + b_ref[...]


def _outproj(ya, yb, yc, x2, wo, g, b, alpha):
    m, d = x2.shape
    tm = 512
    row = lambda w: pl.BlockSpec((tm, w), lambda i: (i, 0))
    full = lambda a: pl.BlockSpec(a.shape, lambda i: (0, 0))
    return pl.pallas_call(
        functools.partial(_outproj_kernel, alpha=alpha),
        grid=(m // tm,),
        in_specs=[row(A_WIDTH), row(B_WIDTH), row(C_WIDTH), row(d), full(wo), full(g), full(b)],
        out_specs=row(d),
        out_shape=jax.ShapeDtypeStruct((m, d), F32),
        compiler_params=pltpu.CompilerParams(dimension_semantics=("arbitrary",)),
        name="out_proj",
    )(ya, yb, yc, x2, wo, g, b)


def _pack_in_weights(w):
    sizes = (A_WIDTH, A_WIDTH, A_WIDTH, A_WIDTH, IDX_HEADS * IDX_DIM, IDX_DIM, IDX_HEADS,
             2 * B_WIDTH, B_WIDTH, C_KEYS, C_KEYS, C_WIDTH, C_WIDTH, GATE_RANK)
    offs = np.concatenate([[0], np.cumsum(sizes)])
    (a_q, a_k, a_v, a_g, i_q, i_k, i_w, b_glu, b_g, c_q, c_k, c_v, c_g, c_lr) = [
        w[:, offs[n]:offs[n + 1]] for n in range(len(sizes))]
    d = w.shape[0]
    pad = lambda a, n: jnp.concatenate([a, jnp.zeros((d, n - a.shape[1]), a.dtype)], axis=1)
    wt = jnp.concatenate([a_q, i_q, a_v, pad(i_w, T_END - T_WI)], axis=1).T
    wr = jnp.concatenate([a_k, pad(i_k, LANES), a_g, b_glu, b_g, c_q, c_k, c_v, c_g,
                          pad(c_lr, LANES)], axis=1)
    return wt.astype(BF16), wr.astype(BF16)


def kernel(x, positions, w_in, conv_w, conv_b, cln_g, cln_b, pw_w, pw_b, gate_w2, gate_b,
           gnorm_g, w_out, ln_g, ln_b):
    batch, seq, d = x.shape
    depth = w_in.shape[0]
    alpha = float((2 * depth) ** 0.25)
    m = batch * seq
    tables = _rope_tables(positions)
    x2 = x.reshape(m, d)
    for l in range(depth):
        wt, wr = _pack_in_weights(w_in[l])
        w2 = jnp.zeros((LANES, C_KEYS), F32).at[:GATE_RANK].set(gate_w2[l])
        w2h = w2.astype(BF16)
        w2l = (w2 - w2h.astype(F32)).astype(BF16)
        (qt, qit, vt, wit, k, ki, sag, glu, sbg, cq, ck, cv, scg, la) = _inproj(
            x2, wt, wr, w2h, w2l, gate_b[l].reshape(1, C_KEYS), tables)
        ya = _attention(qt, qit, wit, sag, k, ki, vt, batch, seq)
        yb = _conformer(glu, sbg, conv_w[l], conv_b[l].reshape(1, -1), cln_g[l].reshape(1, -1),
                        cln_b[l].reshape(1, -1), pw_w[l].astype(BF16), pw_b[l].reshape(1, -1),
                        batch, seq)
        yc = _gla(cq, ck, cv, la, scg, gnorm_g[l].reshape(1, -1), batch, seq)
        x2 = _outproj(ya, yb, yc, x2, w_out[l].astype(BF16), ln_g[l].reshape(1, -1),
                      ln_b[l].reshape(1, -1), alpha)
    return x2.reshape(batch, seq, d)
```

```python
import functools

import numpy as np
import jax
import jax.numpy as jnp
from jax import lax
from jax.experimental import pallas as pl
from jax.experimental.pallas import tpu as pltpu

F32 = jnp.float32
BF16 = jnp.bfloat16
I32 = jnp.int32

A_HEADS = 8
A_HEAD_DIM = 64
A_WIDTH = A_HEADS * A_HEAD_DIM
IDX_HEADS = 8
IDX_DIM = 64
TOPK_MAX = 256
B_WIDTH = 256
CONV_WIDTH = 31
C_HEADS = 4
C_KEY_DIM = 32
C_VAL_DIM = 64
C_KEYS = C_HEADS * C_KEY_DIM
C_WIDTH = C_HEADS * C_VAL_DIM
GATE_RANK = 16
GATE_TAU = 16.0
GLA_CHUNK = 64
GLA_SUB = 16
GLA_EXP_CLAMP = 80.0
ROPE_THETA = 10000.0
ROPE_HALF = 32
EPS = 1e-5

LANES = 128
SUBLANES = 8
VMEM_LIMIT = 56 * 1024 * 1024
NEG_BIG = -1e30
Q_SCALE = A_HEAD_DIM ** -0.5 * float(np.log2(np.e))

R_K, R_KI, R_AG, R_GV, R_GG, R_BG, R_CQ, R_CK, R_CV, R_CG, R_LR, R_END = (
    0, 512, 640, 1152, 1408, 1664, 1920, 2048, 2176, 2432, 2688, 2816)
T_Q, T_QI, T_V, T_WI, T_END = 0, 512, 1024, 1536, 1552


def _dot(a, b):
    return jnp.dot(a, b, preferred_element_type=F32)


def _dot_nt(a, b):
    return lax.dot_general(a, b, (((1,), (1,)), ((), ())), preferred_element_type=F32)


def _sigmoid(x):
    return 1.0 / (1.0 + jnp.exp(-x))


def _silu(x):
    return x * _sigmoid(x)


REDUCE_WAYS = 4


def _col_partial(x, op):
    r, c = x.shape
    return op(x.reshape(r // (8 * REDUCE_WAYS), REDUCE_WAYS, 8, c), axis=0)


def _col_reduce(x, op):
    return op(op(_col_partial(x, op), axis=0), axis=0, keepdims=True)


def _split3(x):
    a = x.astype(BF16)
    r = x - a.astype(F32)
    b = r.astype(BF16)
    c = (r - b.astype(F32)).astype(BF16)
    return a, b, c


def _rope_kernel(post_ref, posc_ref, invc_ref, invr_ref, sgn_ref,
                 cost_ref, sint_ref, cosr_ref, sinr_ref):
    ang_t = invc_ref[...] * post_ref[...]
    cost_ref[...] = jnp.cos(ang_t)
    sint_ref[...] = jnp.sin(ang_t)
    ang_r = posc_ref[...] * invr_ref[...]
    cosr_ref[...] = jnp.cos(ang_r)
    sinr_ref[...] = jnp.sin(ang_r) * sgn_ref[...]


def _rope_tables(positions):
    m = positions.size
    tm = 1024
    pos = positions.reshape(-1).astype(F32)
    inv = ROPE_THETA ** (-jnp.arange(0, 2 * ROPE_HALF, 2, dtype=F32) / (2 * ROPE_HALF))
    lane = np.arange(LANES)
    sgn = jnp.asarray(np.where(lane % 64 < ROPE_HALF, -1.0, 1.0), F32).reshape(1, LANES)
    inv_r = jnp.tile(inv, LANES // ROPE_HALF).reshape(1, LANES)
    return pl.pallas_call(
        _rope_kernel,
        grid=(m // tm,),
        in_specs=[pl.BlockSpec((1, tm), lambda i: (0, i)),
                  pl.BlockSpec((tm, 1), lambda i: (i, 0)),
                  pl.BlockSpec((ROPE_HALF, 1), lambda i: (0, 0)),
                  pl.BlockSpec((1, LANES), lambda i: (0, 0)),
                  pl.BlockSpec((1, LANES), lambda i: (0, 0))],
        out_specs=[pl.BlockSpec((ROPE_HALF, tm), lambda i: (0, i)),
                   pl.BlockSpec((ROPE_HALF, tm), lambda i: (0, i)),
                   pl.BlockSpec((tm, LANES), lambda i: (i, 0)),
                   pl.BlockSpec((tm, LANES), lambda i: (i, 0))],
        out_shape=[jax.ShapeDtypeStruct((ROPE_HALF, m), F32),
                   jax.ShapeDtypeStruct((ROPE_HALF, m), F32),
                   jax.ShapeDtypeStruct((m, LANES), F32),
                   jax.ShapeDtypeStruct((m, LANES), F32)],
        name="rope_tables",
    )(pos.reshape(1, m), pos.reshape(m, 1), inv.reshape(ROPE_HALF, 1), inv_r, sgn)


def _inproj_kernel(x_ref, wt_ref, wr_ref, w2h_ref, w2l_ref, gb_ref,
                   cost_ref, sint_ref, cosr_ref, sinr_ref,
                   qt_ref, qit_ref, vt_ref, wit_ref, k_ref, ki_ref, sag_ref,
                   glu_ref, sbg_ref, cq_ref, ck_ref, cv_ref, scg_ref, la_ref):
    xb = x_ref[...].astype(BF16)
    ct = cost_ref[...]
    st = sint_ref[...]

    ht = _dot_nt(wt_ref[...], xb)
    for h in range(A_HEADS):
        for base, out, scale in ((T_Q, qt_ref, Q_SCALE), (T_QI, qit_ref, 1.0)):
            r0 = base + h * 64
            x1 = ht[r0:r0 + 32]
            x2 = ht[r0 + 32:r0 + 64]
            o0 = h * 64
            out[o0:o0 + 32, :] = ((x1 * ct - x2 * st) * scale).astype(BF16)
            out[o0 + 32:o0 + 64, :] = ((x2 * ct + x1 * st) * scale).astype(BF16)
    vt_ref[...] = ht[T_V:T_V + A_WIDTH].astype(BF16)
    wit_ref[...] = ht[T_WI:T_WI + IDX_HEADS] * (IDX_HEADS ** -0.5 * IDX_DIM ** -0.5)

    cr = cosr_ref[...]
    sr = sinr_ref[...]
    lane = lax.broadcasted_iota(I32, cr.shape, 1)
    first_half = (lane % 64) < ROPE_HALF

    def rope_rows(v):
        fwd = pltpu.roll(v, LANES - ROPE_HALF, 1)
        bwd = pltpu.roll(v, ROPE_HALF, 1)
        return v * cr + jnp.where(first_half, fwd, bwd) * sr

    for g in range(A_WIDTH // LANES):
        c0 = R_K + g * LANES
        hk = _dot(xb, wr_ref[:, c0:c0 + LANES])
        k_ref[:, g * LANES:(g + 1) * LANES] = rope_rows(hk).astype(BF16)
    hki = _dot(xb, wr_ref[:, R_KI:R_KI + LANES])
    ki_ref[...] = rope_rows(hki)[:, :IDX_DIM].astype(BF16)

    sag_ref[...] = _silu(_dot(xb, wr_ref[:, R_AG:R_AG + A_WIDTH])).astype(BF16)
    gv = _dot(xb, wr_ref[:, R_GV:R_GV + B_WIDTH])
    gg = _dot(xb, wr_ref[:, R_GG:R_GG + B_WIDTH])
    glu_ref[...] = gv * _sigmoid(gg)
    sbg_ref[...] = _silu(_dot(xb, wr_ref[:, R_BG:R_BG + B_WIDTH])).astype(BF16)
    cq_ref[...] = _dot(xb, wr_ref[:, R_CQ:R_CQ + C_KEYS]) * (C_KEY_DIM ** -0.5)
    ck_ref[...] = _dot(xb, wr_ref[:, R_CK:R_CK + C_KEYS])
    cv_ref[...] = _dot(xb, wr_ref[:, R_CV:R_CV + C_WIDTH])
    scg_ref[...] = _silu(_dot(xb, wr_ref[:, R_CG:R_CG + C_WIDTH])).astype(BF16)

    lr = _dot(xb, wr_ref[:, R_LR:R_LR + LANES])
    lr_h = lr.astype(BF16)
    lr_l = (lr - lr_h.astype(F32)).astype(BF16)
    z = (_dot(lr_h, w2h_ref[...]) + _dot(lr_l, w2h_ref[...]) + _dot(lr_h, w2l_ref[...])
         + gb_ref[...])
    log_sig = jnp.minimum(z, 0.0) - jnp.log(1.0 + jnp.exp(-jnp.abs(z)))
    la_ref[...] = log_sig * (1.0 / GATE_TAU)


def _inproj(x2, wt, wr, w2h, w2l, gb, tables):
    m, d = x2.shape
    tm = 512
    cost, sint, cosr, sinr = tables
    row = lambda w: pl.BlockSpec((tm, w), lambda i: (i, 0))
    col = lambda r: pl.BlockSpec((r, tm), lambda i: (0, i))
    full = lambda a: pl.BlockSpec(a.shape, lambda i: (0, 0))
    outs = [(col(A_WIDTH), (A_WIDTH, m), BF16),
            (col(A_WIDTH), (A_WIDTH, m), BF16),
            (col(A_WIDTH), (A_WIDTH, m), BF16),
            (col(IDX_HEADS), (IDX_HEADS, m), F32),
            (row(A_WIDTH), (m, A_WIDTH), BF16),
            (row(IDX_DIM), (m, IDX_DIM), BF16),
            (row(A_WIDTH), (m, A_WIDTH), BF16),
            (row(B_WIDTH), (m, B_WIDTH), F32),
            (row(B_WIDTH), (m, B_WIDTH), BF16),
            (row(C_KEYS), (m, C_KEYS), F32),
            (row(C_KEYS), (m, C_KEYS), F32),
            (row(C_WIDTH), (m, C_WIDTH), F32),
            (row(C_WIDTH), (m, C_WIDTH), BF16),
            (row(C_KEYS), (m, C_KEYS), F32)]
    return pl.pallas_call(
        _inproj_kernel,
        grid=(m // tm,),
        in_specs=[row(d), full(wt), full(wr), full(w2h), full(w2l), full(gb),
                  col(ROPE_HALF), col(ROPE_HALF), row(LANES), row(LANES)],
        out_specs=[o[0] for o in outs],
        out_shape=[jax.ShapeDtypeStruct(o[1], o[2]) for o in outs],
        compiler_params=pltpu.CompilerParams(
            dimension_semantics=("arbitrary",), vmem_limit_bytes=VMEM_LIMIT),
        name="in_proj",
    )(x2, wt, wr, w2h, w2l, gb, cost, sint, cosr, sinr)


KEY_LOWEST_FINITE = -2139095040
ATTN_TQ = 256
ATTN_TK = 256


def _key_to_float(key):
    return pltpu.bitcast(key ^ ((key >> 31) & 0x7FFFFFFF), F32)


def _attn_kernel(qt_ref, qit_ref, wit_ref, sag_ref, k_ref, ki_ref, vt_ref,
                 o_ref, sc_ref, hi_ref, qm_ref, acc_ref, tie_ref, s_ref, p_ref, *, tq, tk, topk, seq):
    i = pl.program_id(1)
    nchunks = (i + 1) * (tq // tk)
    rel = (lax.broadcasted_iota(I32, (tk, tq), 0) - lax.broadcasted_iota(I32, (tk, tq), 1))

    def score_pair(t, carry):
        js = (2 * t, 2 * t + 1)
        starts = [pl.multiple_of(j * tk, tk) for j in js]
        kics = [ki_ref[pl.ds(ks, tk), :] for ks in starts]
        scs = [jnp.zeros((tk, tq), F32) for _ in js]
        for h in range(IDX_HEADS):
            for n in range(2):
                lg = _dot(kics[n], qit_ref[h * IDX_DIM:(h + 1) * IDX_DIM, :])
                scs[n] = scs[n] + wit_ref[h:h + 1, :] * jnp.maximum(lg, 0.0)
        for n in range(2):
            sc = jnp.where(rel <= i * tq - js[n] * tk, scs[n], -jnp.inf)
            sc_ref[pl.ds(starts[n], tk), :] = sc
            upper = pltpu.bitcast(pltpu.bitcast(sc, I32) & -65536, F32)
            hi_ref[pl.ds(starts[n], tk), :] = upper.astype(BF16)
        return carry

    lax.fori_loop(0, (nchunks + 1) // 2, score_pair, 0)

    packed_rows = 16
    one16 = jnp.ones((tk, tq), BF16)
    zero16 = jnp.zeros((tk, tq), BF16)

    def count_upper(cand):
        def body(t, acc):
            for j in (2 * t, 2 * t + 1):
                ks = pl.multiple_of(j * tk, tk)
                hit = jnp.where(hi_ref[pl.ds(ks, tk), :] >= cand, one16, zero16)
                hit = hit.reshape(tk // (packed_rows * REDUCE_WAYS), REDUCE_WAYS, packed_rows, tq)
                part = hit[0]
                for r in range(1, hit.shape[0]):
                    part = part + hit[r]
                acc = acc + part.astype(F32)
            return acc
        acc = lax.fori_loop(0, (nchunks + 1) // 2, body,
                            jnp.zeros((REDUCE_WAYS, packed_rows, tq), F32))
        return acc.sum(axis=0).sum(axis=0, keepdims=True).astype(I32)

    def upper_key_to_bf16(key16):
        pattern = (key16 ^ ((key16 >> 31) & 0x7FFF)) & 0xFFFF
        return pltpu.bitcast(jnp.left_shift(pattern, 16), F32).astype(BF16)

    def count_ge(cand):
        def body(t, acc):
            for j in (2 * t, 2 * t + 1):
                ks = pl.multiple_of(j * tk, tk)
                hit = jnp.where(sc_ref[pl.ds(ks, tk), :] >= cand, 1, 0)
                acc = acc + _col_partial(hit, jnp.sum)
            return acc
        acc = lax.fori_loop(0, (nchunks + 1) // 2, body, jnp.zeros((REDUCE_WAYS, 8, tq), I32))
        return acc.sum(axis=0).sum(axis=0, keepdims=True)

    zero = jnp.zeros((1, tq), I32)
    cnt0 = count_upper(jnp.zeros((1, tq), BF16))
    tau = jnp.where(cnt0 >= topk, zero, -32768)
    cnt_tau = jnp.where(cnt0 >= topk, cnt0, zero)

    def upper_step(it, carry):
        tau, cnt_tau = carry
        cand = tau + jnp.left_shift(jnp.int32(1), 14 - it)
        cnt = count_upper(upper_key_to_bf16(cand))
        take = cnt >= topk
        return jnp.where(take, cand, tau), jnp.where(take, cnt, cnt_tau)

    tau, cnt_tau = lax.fori_loop(0, 15, upper_step, (tau, cnt_tau))
    tau = jnp.left_shift(tau, 16)

    def lower_step(it, carry):
        tau, cnt_tau = carry
        cand = tau + jnp.left_shift(jnp.int32(1), 15 - it)
        cnt = count_ge(_key_to_float(cand))
        take = cnt >= topk
        return jnp.where(take, cand, tau), jnp.where(take, cnt, cnt_tau)

    tau, cnt_tau = lax.fori_loop(0, 16, lower_step, (tau, cnt_tau))
    thr = _key_to_float(jnp.maximum(tau, KEY_LOWEST_FINITE))

    has_ties = jnp.max(cnt_tau) > topk

    @pl.when(jnp.logical_not(has_ties))
    def _():
        def bias_chunk(j, carry):
            ks = pl.multiple_of(j * tk, tk)
            sc_ref[pl.ds(ks, tk), :] = jnp.where(sc_ref[pl.ds(ks, tk), :] >= thr, 0.0, NEG_BIG)
            return carry

        lax.fori_loop(0, nchunks, bias_chunk, 0)

    @pl.when(has_ties)
    def _():
        npairs = (nchunks + 1) // 2

        def mark(t, total):
            for j in (2 * t, 2 * t + 1):
                ks = pl.multiple_of(j * tk, tk)
                tie = jnp.where(sc_ref[pl.ds(ks, tk), :] == thr, 1.0, 0.0)
                hi_ref[pl.ds(ks, tk), :] = tie.astype(BF16)
                part = _col_partial(tie, jnp.sum).sum(axis=0)
                tie_ref[j] = part
                total = total + part
            return total

        n_tie = lax.fori_loop(0, npairs, mark, jnp.zeros((8, tq), F32)).sum(axis=0, keepdims=True)
        need = (topk - cnt_tau).astype(F32) + n_tie
        r_i = lax.broadcasted_iota(I32, (tk, tk), 0)
        c_i = lax.broadcasted_iota(I32, (tk, tk), 1)
        upto = jnp.where(c_i <= r_i, 1.0, 0.0).astype(BF16)

        def bias_pair(t, before):
            starts = [pl.multiple_of(j * tk, tk) for j in (2 * t, 2 * t + 1)]
            inside = [_dot(upto, hi_ref[pl.ds(ks, tk), :]) for ks in starts]
            for n, ks in enumerate(starts):
                s = sc_ref[pl.ds(ks, tk), :]
                tied = jnp.where(before + inside[n] <= need, 0.0, NEG_BIG)
                sc_ref[pl.ds(ks, tk), :] = jnp.where(s > thr, 0.0,
                                                     jnp.where(s == thr, tied, NEG_BIG))
                before = before + tie_ref[2 * t + n].sum(axis=0, keepdims=True)
            return before

        lax.fori_loop(0, npairs, bias_pair, jnp.zeros((1, tq), F32))

    head_row = lax.broadcasted_iota(I32, (LANES, tq), 0) // A_HEAD_DIM
    for h in range(A_HEADS):
        qpair = qt_ref[(h // 2) * LANES:(h // 2 + 1) * LANES, :]
        qm_ref[h] = jnp.where(head_row == (h % 2), qpair, jnp.zeros_like(qpair))
    acc_ref[...] = jnp.zeros(acc_ref.shape, F32)

    last_chunk = seq // tk - 1

    @pl.when((nchunks % 2 == 1) & (nchunks <= last_chunk))
    def _():
        sc_ref[pl.ds(pl.multiple_of(nchunks * tk, tk), tk), :] = jnp.full((tk, tq), NEG_BIG, F32)

    def step(j, slot, m_run, l_run, top_cur, alpha_prev):
        other = 1 - slot
        ks_a = pl.multiple_of(jnp.minimum(j + 1, last_chunk) * tk, tk)
        ks_c = pl.multiple_of(jnp.maximum(j - 1, 0) * tk, tk)
        m_new = jnp.maximum(m_run, top_cur)
        alpha = jnp.exp2(m_run - m_new)
        tops, sums = [], []
        for h in range(A_HEADS):
            hs = slice(h * A_HEAD_DIM, (h + 1) * A_HEAD_DIM)
            kc = k_ref[pl.ds(ks_a, tk), (h // 2) * LANES:(h // 2 + 1) * LANES]
            s = _dot(kc, qm_ref[h]) + sc_ref[pl.ds(ks_a, tk), :]
            s_ref[other * A_HEADS + h] = s
            tops.append(_col_reduce(s, jnp.max))
            pv = _dot(vt_ref[hs, pl.ds(ks_c, tk)], p_ref[other * A_HEADS + h])
            acc_ref[hs, :] = alpha_prev[h:h + 1, :] * acc_ref[hs, :] + pv
            p = jnp.exp2(s_ref[slot * A_HEADS + h] - m_new[h:h + 1, :])
            p_ref[slot * A_HEADS + h] = p.astype(BF16)
            sums.append(_col_reduce(p, jnp.sum))
        l_new = alpha * l_run + jnp.concatenate(sums, axis=0)
        return m_new, l_new, jnp.concatenate(tops, axis=0), alpha

    def attn_pair(t, carry):
        carry = step(2 * t, 0, *carry)
        return step(2 * t + 1, 1, *carry)

    tops0 = []
    for h in range(A_HEADS):
        s = _dot(k_ref[0:tk, (h // 2) * LANES:(h // 2 + 1) * LANES], qm_ref[h]) + sc_ref[0:tk, :]
        s_ref[h] = s
        tops0.append(_col_reduce(s, jnp.max))
    p_ref[A_HEADS:2 * A_HEADS] = jnp.zeros((A_HEADS, tk, tq), BF16)
    init = (jnp.full((A_HEADS, tq), NEG_BIG, F32), jnp.zeros((A_HEADS, tq), F32),
            jnp.concatenate(tops0, axis=0), jnp.ones((A_HEADS, tq), F32))
    npairs = (nchunks + 1) // 2
    _, l_fin, _, alpha_last = lax.fori_loop(0, npairs, attn_pair, init)
    ks_last = pl.multiple_of(jnp.minimum(2 * npairs - 1, last_chunk) * tk, tk)
    for h in range(A_HEADS):
        hs = slice(h * A_HEAD_DIM, (h + 1) * A_HEAD_DIM)
        pv = _dot(vt_ref[hs, pl.ds(ks_last, tk)], p_ref[A_HEADS + h])
        acc_ref[hs, :] = alpha_last[h:h + 1, :] * acc_ref[hs, :] + pv

    for h in range(A_HEADS):
        hs = slice(h * A_HEAD_DIM, (h + 1) * A_HEAD_DIM)
        acc_ref[hs, :] = acc_ref[hs, :] / l_fin[h:h + 1, :]
    o_ref[...] = (acc_ref[...].T * sag_ref[...].astype(F32)).astype(BF16)


def _attention(qt, qit, wit, sag, k, ki, vt, batch, seq):
    tq, tk = ATTN_TQ, ATTN_TK
    assert seq % (2 * tk) == 0 and tq % tk == 0
    nq = seq // tq
    topk = min(TOPK_MAX, seq // 4)
    once = pl.Buffered(1)
    return pl.pallas_call(
        functools.partial(_attn_kernel, tq=tq, tk=tk, topk=topk, seq=seq),
        grid=(batch, nq),
        in_specs=[pl.BlockSpec((A_WIDTH, tq), lambda b, i: (0, b * nq + i)),
                  pl.BlockSpec((A_WIDTH, tq), lambda b, i: (0, b * nq + i)),
                  pl.BlockSpec((IDX_HEADS, tq), lambda b, i: (0, b * nq + i)),
                  pl.BlockSpec((tq, A_WIDTH), lambda b, i: (b * nq + i, 0)),
                  pl.BlockSpec((seq, A_WIDTH), lambda b, i: (b, 0), pipeline_mode=once),
                  pl.BlockSpec((seq, IDX_DIM), lambda b, i: (b, 0), pipeline_mode=once),
                  pl.BlockSpec((A_WIDTH, seq), lambda b, i: (0, b), pipeline_mode=once)],
        out_specs=pl.BlockSpec((tq, A_WIDTH), lambda b, i: (b * nq + i, 0)),
        out_shape=jax.ShapeDtypeStruct((batch * seq, A_WIDTH), BF16),
        scratch_shapes=[pltpu.VMEM((seq, tq), F32),
                        pltpu.VMEM((seq, tq), BF16),
                        pltpu.VMEM((A_HEADS, LANES, tq), BF16),
                        pltpu.VMEM((A_WIDTH, tq), F32),
                        pltpu.VMEM((seq // tk, SUBLANES, tq), F32),
                        pltpu.VMEM((2 * A_HEADS, tk, tq), F32),
                        pltpu.VMEM((2 * A_HEADS, tk, tq), BF16)],
        compiler_params=pltpu.CompilerParams(
            dimension_semantics=("arbitrary", "arbitrary"), vmem_limit_bytes=VMEM_LIMIT),
        name="dsa_attention",
    )(qt, qit, wit, sag, k, ki, vt)


CONV_HALO = 32
CONV_SUB = 64


def _conv_kernel(cur_ref, halo_ref, sbg_ref, cw_ref, cb_ref, lg_ref, lb_ref, pw_ref, pb_ref,
                 o_ref, buf_ref, *, tt):
    i = pl.program_id(1)
    halo = halo_ref[...]
    buf_ref[0, 0:CONV_HALO, :] = jnp.where(i == 0, jnp.zeros_like(halo), halo)
    buf_ref[0, CONV_HALO:CONV_HALO + tt, :] = cur_ref[...]
    lead = CONV_HALO - (CONV_WIDTH - 1)
    span = tt + CONV_HALO - SUBLANES
    for s in range(1, SUBLANES):
        buf_ref[s, 0:span, :] = buf_ref[0, s:s + span, :]
    for r in range(tt // CONV_SUB):
        acc = jnp.zeros((CONV_SUB, B_WIDTH), F32)
        for j in range(CONV_WIDTH):
            shift, base = (lead + j) % SUBLANES, (lead + j) // SUBLANES * SUBLANES
            start = r * CONV_SUB + base
            acc = acc + buf_ref[shift, start:start + CONV_SUB, :] * cw_ref[j:j + 1, :]
        hc = acc + cb_ref[...]
        mu = jnp.mean(hc, axis=-1, keepdims=True)
        var = jnp.mean(jnp.square(hc - mu), axis=-1, keepdims=True)
        hn = (hc - mu) * lax.rsqrt(var + EPS) * lg_ref[...] + lb_ref[...]
        act = _silu(hn).astype(BF16)
        yb = _dot(act, pw_ref[...]) + pb_ref[...]
        rows = slice(r * CONV_SUB, (r + 1) * CONV_SUB)
        o_ref[rows, :] = (yb * sbg_ref[rows, :].astype(F32)).astype(BF16)


def _conformer(glu, sbg, cw, cb, lg, lb, pw, pb, batch, seq):
    tt = 512
    nt = seq // tt
    per = tt // CONV_HALO
    vec = lambda a: pl.BlockSpec(a.shape, lambda b, i: (0, 0))
    return pl.pallas_call(
        functools.partial(_conv_kernel, tt=tt),
        grid=(batch, nt),
        in_specs=[pl.BlockSpec((tt, B_WIDTH), lambda b, i: (b * nt + i, 0)),
                  pl.BlockSpec((CONV_HALO, B_WIDTH),
                               lambda b, i: (jnp.maximum((b * nt + i) * per - 1, 0), 0)),
                  pl.BlockSpec((tt, B_WIDTH), lambda b, i: (b * nt + i, 0)),
                  vec(cw), vec(cb), vec(lg), vec(lb), vec(pw), vec(pb)],
        out_specs=pl.BlockSpec((tt, B_WIDTH), lambda b, i: (b * nt + i, 0)),
        out_shape=jax.ShapeDtypeStruct((batch * seq, B_WIDTH), BF16),
        scratch_shapes=[pltpu.VMEM((SUBLANES, CONV_HALO + tt, B_WIDTH), F32)],
        compiler_params=pltpu.CompilerParams(dimension_semantics=("arbitrary", "arbitrary")),
        name="conformer_conv",
    )(glu, glu, sbg, cw, cb, lg, lb, pw, pb)


def _gla_kernel(q_ref, k_ref, v_ref, g_ref, scg_ref, gn_ref, o_ref, st_ref, *, tg, nbatch):
    c = GLA_CHUNK
    nsub = c // GLA_SUB

    @pl.when(pl.program_id(0) == 0)
    def _():
        st_ref[...] = jnp.zeros_like(st_ref)

    ri = lax.broadcasted_iota(I32, (c, c), 0)
    ci = lax.broadcasted_iota(I32, (c, c), 1)
    tri = jnp.where(ri >= ci, 1.0, 0.0).astype(BF16)
    r2 = lax.broadcasted_iota(I32, (C_WIDTH, C_KEYS), 0)
    c2 = lax.broadcasted_iota(I32, (C_WIDTH, C_KEYS), 1)
    key_blk = (r2 // C_VAL_DIM) == (c2 // C_KEY_DIM)
    key_blk_sub = jnp.concatenate([key_blk] * nsub, axis=1)
    r3 = lax.broadcasted_iota(I32, (C_WIDTH, C_WIDTH), 0)
    c3 = lax.broadcasted_iota(I32, (C_WIDTH, C_WIDTH), 1)
    val_blk = (r3 // C_VAL_DIM) == (c3 // C_VAL_DIM)
    ones_blk = jnp.where(val_blk, 1.0, 0.0).astype(BF16)
    r4 = lax.broadcasted_iota(I32, (c, C_WIDTH), 0)
    c4 = lax.broadcasted_iota(I32, (c, C_WIDTH), 1)
    causal = (c4 % c) <= r4
    sub_of_row = lax.broadcasted_iota(I32, (c, C_KEYS), 0) // GLA_SUB

    def one_batch(bi, rows):
        q = q_ref[bi, rows, :]
        k = k_ref[bi, rows, :]
        v = v_ref[bi, rows, :]
        g1, g2, g3 = _split3(g_ref[bi, rows, :])
        b = _dot(tri, g1) + _dot(tri, g2) + _dot(tri, g3)
        yield
        b_last = b[c - 1:c, :]
        qe = (q * jnp.exp(b)).astype(BF16)
        kd = (k * jnp.exp(b_last - b)).astype(BF16)
        st = st_ref[bi]
        yield

        heads = [b[s * GLA_SUB:s * GLA_SUB + 1, :] for s in range(nsub)]
        own = heads[0]
        for s in range(1, nsub):
            own = jnp.where(sub_of_row >= s, heads[s], own)
        qx = q * jnp.exp(b - own)
        yield
        q_cat = jnp.concatenate([jnp.where(sub_of_row == s, qx, 0.0) for s in range(nsub)],
                                axis=1).astype(BF16)
        k_cat = jnp.concatenate(
            [jnp.where(sub_of_row <= s, k * jnp.exp(jnp.minimum(heads[s] - b, GLA_EXP_CLAMP)), 0.0)
             for s in range(nsub)], axis=1)
        yield
        k_bd = jnp.where(key_blk_sub, jnp.concatenate([k_cat] * C_HEADS, axis=0), 0.0)
        yield
        a = jnp.where(causal, _dot_nt(q_cat, k_bd.astype(BF16)), 0.0)
        yield
        v_bd = jnp.where(val_blk, jnp.concatenate([v] * C_HEADS, axis=0), 0.0).astype(BF16)
        o = _dot(a.astype(BF16), v_bd) + _dot_nt(qe, st.astype(BF16))

        yield
        upd = _dot(v.T.astype(BF16), kd)
        st_ref[bi] = st * jnp.exp(b_last) + jnp.where(key_blk, upd, 0.0)

        yield
        oh, ol, _ = _split3(o * o)
        ms = (_dot(oh, ones_blk) + _dot(ol, ones_blk)) * (1.0 / C_VAL_DIM)
        yield
        on = o * lax.rsqrt(ms + EPS) * gn_ref[...]
        o_ref[bi, rows, :] = (on * scg_ref[bi, rows, :].astype(F32)).astype(BF16)

    def chunk(n, carry):
        rows = pl.ds(pl.multiple_of(n * c, c), c)
        pending = [one_batch(bi, rows) for bi in range(nbatch)]
        while pending:
            pending = [gen for gen in pending if next(gen, StopIteration) is not StopIteration]
        return carry

    lax.fori_loop(0, tg // c, chunk, 0)


def _gla(cq, ck, cv, la, scg, gn, batch, seq):
    tg = 512
    blk = lambda w: pl.BlockSpec((batch, tg, w), lambda i: (0, i, 0))
    by_batch = lambda a: a.reshape(batch, seq, a.shape[-1])
    out = pl.pallas_call(
        functools.partial(_gla_kernel, tg=tg, nbatch=batch),
        grid=(seq // tg,),
        in_specs=[blk(C_KEYS), blk(C_KEYS), blk(C_WIDTH), blk(C_KEYS), blk(C_WIDTH),
                  pl.BlockSpec(gn.shape, lambda i: (0, 0))],
        out_specs=blk(C_WIDTH),
        out_shape=jax.ShapeDtypeStruct((batch, seq, C_WIDTH), BF16),
        scratch_shapes=[pltpu.VMEM((batch, C_WIDTH, C_KEYS), F32)],
        compiler_params=pltpu.CompilerParams(dimension_semantics=("arbitrary",)),
        name="gla",
    )(by_batch(cq), by_batch(ck), by_batch(cv), by_batch(la), by_batch(scg), gn)
    return out.reshape(batch * seq, C_WIDTH)


def _outproj_kernel(ya_ref, yb_ref, yc_ref, x_ref, wo_ref, g_ref, b_ref, o_ref, *, alpha):
    d = o_ref.shape[1]
    half = d // 2
    total = jnp.zeros((o_ref.shape[0], 1), F32)
    for c0 in (0, half):
        cols = slice(c0, c0 + half)
        y = (_dot(ya_ref[...], wo_ref[0:A_WIDTH, cols])
             + _dot(yb_ref[...], wo_ref[A_WIDTH:A_WIDTH + B_WIDTH, cols])
             + _dot(yc_ref[...], wo_ref[A_WIDTH + B_WIDTH:, cols]))
        z = alpha * x_ref[:, cols] + y
        o_ref[:, cols] = z
        total = total + jnp.sum(z, axis=-1, keepdims=True)
    mu = total * (1.0 / d)
    sq = jnp.zeros_like(total)
    for c0 in (0, half):
        cols = slice(c0, c0 + half)
        sq = sq + jnp.sum(jnp.square(o_ref[:, cols] - mu), axis=-1, keepdims=True)
    inv = lax.rsqrt(sq * (1.0 / d) + EPS)
    for c0 in (0, half):
        cols = slice(c0, c0 + half)
        o_ref[:, cols] = (o_ref[:, cols] - mu) * inv * g_ref[:, cols] + b_ref[:, cols]


def _outproj(ya, yb, yc, x2, wo, g, b, alpha):
    m, d = x2.shape
    tm = 512
    row = lambda w: pl.BlockSpec((tm, w), lambda i: (i, 0))
    full = lambda a: pl.BlockSpec(a.shape, lambda i: (0, 0))
    return pl.pallas_call(
        functools.partial(_outproj_kernel, alpha=alpha),
        grid=(m // tm,),
        in_specs=[row(A_WIDTH), row(B_WIDTH), row(C_WIDTH), row(d), full(wo), full(g), full(b)],
        out_specs=row(d),
        out_shape=jax.ShapeDtypeStruct((m, d), F32),
        compiler_params=pltpu.CompilerParams(dimension_semantics=("arbitrary",)),
        name="out_proj",
    )(ya, yb, yc, x2, wo, g, b)


def _pack_in_weights(w):
    sizes = (A_WIDTH, A_WIDTH, A_WIDTH, A_WIDTH, IDX_HEADS * IDX_DIM, IDX_DIM, IDX_HEADS,
             2 * B_WIDTH, B_WIDTH, C_KEYS, C_KEYS, C_WIDTH, C_WIDTH, GATE_RANK)
    offs = np.concatenate([[0], np.cumsum(sizes)])
    (a_q, a_k, a_v, a_g, i_q, i_k, i_w, b_glu, b_g, c_q, c_k, c_v, c_g, c_lr) = [
        w[:, offs[n]:offs[n + 1]] for n in range(len(sizes))]
    d = w.shape[0]
    pad = lambda a, n: jnp.concatenate([a, jnp.zeros((d, n - a.shape[1]), a.dtype)], axis=1)
    wt = jnp.concatenate([a_q, i_q, a_v, pad(i_w, T_END - T_WI)], axis=1).T
    wr = jnp.concatenate([a_k, pad(i_k, LANES), a_g, b_glu, b_g, c_q, c_k, c_v, c_g,
                          pad(c_lr, LANES)], axis=1)
    return wt.astype(BF16), wr.astype(BF16)


def kernel(x, positions, w_in, conv_w, conv_b, cln_g, cln_b, pw_w, pw_b, gate_w2, gate_b,
           gnorm_g, w_out, ln_g, ln_b):
    batch, seq, d = x.shape
    depth = w_in.shape[0]
    alpha = float((2 * depth) ** 0.25)
    m = batch * seq
    tables = _rope_tables(positions)
    x2 = x.reshape(m, d)
    for l in range(depth):
        wt, wr = _pack_in_weights(w_in[l])
        w2 = jnp.zeros((LANES, C_KEYS), F32).at[:GATE_RANK].set(gate_w2[l])
        w2h = w2.astype(BF16)
        w2l = (w2 - w2h.astype(F32)).astype(BF16)
        (qt, qit, vt, wit, k, ki, sag, glu, sbg, cq, ck, cv, scg, la) = _inproj(
            x2, wt, wr, w2h, w2l, gate_b[l].reshape(1, C_KEYS), tables)
        ya = _attention(qt, qit, wit, sag, k, ki, vt, batch, seq)
        yb = _conformer(glu, sbg, conv_w[l], conv_b[l].reshape(1, -1), cln_g[l].reshape(1, -1),
                        cln_b[l].reshape(1, -1), pw_w[l].astype(BF16), pw_b[l].reshape(1, -1),
                        batch, seq)
        yc = _gla(cq, ck, cv, la, scg, gnorm_g[l].reshape(1, -1), batch, seq)
        x2 = _outproj(ya, yb, yc, x2, w_out[l].astype(BF16), ln_g[l].reshape(1, -1),
                      ln_b[l].reshape(1, -1), alpha)
    return x2.reshape(batch, seq, d)
```
